```python
import jax, jax.numpy as jnp
from jax import lax
import numpy as np

D_MODEL = 4096
BATCH = 1
SEQ = 8192
DEPTH = 1

CHUNK = 64
Q_BLOCK = 128
ROPE_THETA = 10000.0
NORM_EPS = 1e-6
ATTN_WIDTH = D_MODEL // 2
ATTN_HEAD_DIM = 128
ATTN_HEADS = ATTN_WIDTH // ATTN_HEAD_DIM
IDX_HEADS = 32
IDX_HEAD_DIM = 64
TOPK_MAX = 256
SSD_WIDTH = D_MODEL // 2
SSD_HEAD_DIM = 64
SSD_HEADS = SSD_WIDTH // SSD_HEAD_DIM
SSD_GROUPS = 8
SSD_STATE = 128
CONV_WIDTH = 4
CONV_CH = SSD_WIDTH + 2 * SSD_GROUPS * SSD_STATE
N_BRANCHES = 2
SPLIT_SIZES = (
    ATTN_WIDTH, ATTN_WIDTH, ATTN_WIDTH, ATTN_WIDTH,
    IDX_HEADS * IDX_HEAD_DIM, IDX_HEAD_DIM, IDX_HEADS,
    SSD_WIDTH, SSD_WIDTH, SSD_GROUPS * SSD_STATE, SSD_GROUPS * SSD_STATE, SSD_HEADS,
    N_BRANCHES * D_MODEL,
)
IN_WIDTH = sum(SPLIT_SIZES)

kernel_name = "hybrid_dsa_ssd_gated_merge_block"


def rms_norm(x, gain):
    xf = x.astype(jnp.float32)
    y = xf * lax.rsqrt(jnp.mean(xf * xf, axis=-1, keepdims=True) + NORM_EPS)
    return (y * gain.astype(jnp.float32)).astype(x.dtype)


def rope_tables(seq_len, dim):
    inv_freq = 1.0 / (ROPE_THETA ** (jnp.arange(0, dim, 2, dtype=jnp.float32) / dim))
    ang = jnp.arange(seq_len, dtype=jnp.float32)[:, None] * inv_freq[None, :]
    return jnp.cos(ang), jnp.sin(ang)


def apply_rope(x, cos, sin):
    xf = x.astype(jnp.float32)
    x1, x2 = jnp.split(xf, 2, axis=-1)
    c = cos[None, :, None, :]
    s = sin[None, :, None, :]
    return jnp.concatenate([x1 * c - x2 * s, x2 * c + x1 * s], axis=-1).astype(x.dtype)


def dsa_attention(q, k, v, qi, ki, wi):
    b, s = q.shape[0], q.shape[1]
    topk = min(TOPK_MAX, s // 4)
    nb = s // Q_BLOCK
    scale = ATTN_HEAD_DIM ** -0.5
    key_pos = jnp.arange(s)
    ki32 = ki.astype(jnp.float32)
    gather = jax.vmap(lambda table, idx: table[idx])

    def to_blocks(t):
        return jnp.moveaxis(t.reshape((b, nb, Q_BLOCK) + t.shape[2:]), 1, 0)

    def one_block(args):
        qb, qib, wib, blk = args
        q_pos = blk * Q_BLOCK + jnp.arange(Q_BLOCK)
        visible_end = (q_pos // CHUNK + 1) * CHUNK
        admissible = key_pos[None, :] < visible_end[:, None]
        idx_logits = jnp.einsum("bthd,bsd->bths", qib.astype(jnp.float32), ki32)
        idx_score = jnp.einsum("bths,bth->bts", jax.nn.relu(idx_logits), wib.astype(jnp.float32))
        idx_score = jnp.where(admissible[None], idx_score, -jnp.inf)
        _, sel = lax.top_k(idx_score, topk)
        sel_valid = sel < visible_end[None, :, None]
        k_sel = gather(k, sel).astype(jnp.float32)
        v_sel = gather(v, sel).astype(jnp.float32)
        logits = jnp.einsum("bthd,btkhd->bthk", qb.astype(jnp.float32), k_sel) * scale
        logits = jnp.where(sel_valid[:, :, None, :], logits, -jnp.inf)
        p = jax.nn.softmax(logits, axis=-1)
        return jnp.einsum("bthk,btkhd->bthd", p, v_sel).astype(q.dtype)

    out = lax.map(one_block, (to_blocks(q), to_blocks(qi), to_blocks(wi), jnp.arange(nb)))
    return jnp.moveaxis(out, 0, 1).reshape(q.shape)


def ssd_mixer(xs, z, bm, cm, dt_raw, conv_w, conv_b, dt_bias, a_log, d_skip, norm_gain):
    b, s, _ = xs.shape
    g, r, p, n = SSD_GROUPS, SSD_HEADS // SSD_GROUPS, SSD_HEAD_DIM, SSD_STATE
    nc = s // CHUNK
    xbc = jnp.concatenate([xs, bm, cm], axis=-1)
    xbc = lax.conv_general_dilated(
        xbc, conv_w[:, None, :].astype(xbc.dtype), window_strides=(1,),
        padding=[(CONV_WIDTH - 1, 0)], dimension_numbers=("NWC", "WIO", "NWC"),
        feature_group_count=CONV_CH) + conv_b
    xbc = jax.nn.silu(xbc)
    xs_c, bm_c, cm_c = jnp.split(xbc, [SSD_WIDTH, SSD_WIDTH + g * n], axis=-1)

    dt = jax.nn.softplus(dt_raw.astype(jnp.float32) + dt_bias.astype(jnp.float32))
    a = -jnp.exp(a_log.astype(jnp.float32))
    x = xs_c.astype(jnp.float32).reshape(b, nc, CHUNK, g, r, p)
    bc = bm_c.astype(jnp.float32).reshape(b, nc, CHUNK, g, n)
    cc = cm_c.astype(jnp.float32).reshape(b, nc, CHUNK, g, n)
    dtc = dt.reshape(b, nc, CHUNK, g, r)
    a_cum = jnp.cumsum(dtc * a.reshape(g, r), axis=2)
    xdt = x * dtc[..., None]

    a_t = jnp.moveaxis(a_cum, 2, -1)
    seg = a_t[..., :, None] - a_t[..., None, :]
    causal = jnp.tril(jnp.ones((CHUNK, CHUNK), dtype=bool))
    decay = jnp.exp(jnp.where(causal, seg, -jnp.inf))
    cb = jnp.einsum("bclgn,bcsgn->bcgls", cc, bc)
    y_diag = jnp.einsum("bcgrls,bcsgrp->bclgrp", cb[:, :, :, None] * decay, xdt)

    decay_to_end = jnp.exp(a_cum[:, :, -1:] - a_cum)
    states = jnp.einsum("bclgn,bclgrp->bcgrpn", bc, xdt * decay_to_end[..., None])
    chunk_decay = jnp.exp(a_cum[:, :, -1])

    def step(h, inp):
        st, dc = inp
        return h * dc[..., None, None] + st, h

    h0 = jnp.zeros((b, g, r, p, n), jnp.float32)
    _, h_in = lax.scan(step, h0, (jnp.moveaxis(states, 1, 0), jnp.moveaxis(chunk_decay, 1, 0)))
    h_in = jnp.moveaxis(h_in, 0, 1)
    y_off = jnp.einsum("bclgn,bcgrpn->bclgrp", cc, h_in) * jnp.exp(a_cum)[..., None]

    y = y_diag + y_off + x * d_skip.astype(jnp.float32).reshape(g, r)[:, :, None]
    y = y.reshape(b, s, SSD_WIDTH)
    yg = (y * jax.nn.silu(z.astype(jnp.float32))).reshape(b, s, g, SSD_WIDTH // g)
    yg = yg * lax.rsqrt(jnp.mean(yg * yg, axis=-1, keepdims=True) + NORM_EPS)
    return (yg.reshape(b, s, SSD_WIDTH) * norm_gain.astype(jnp.float32)).astype(xs.dtype)


def setup_inputs(seed: int = 0) -> dict:
    key = jax.random.key(seed)
    ks = jax.random.split(key, 13)
    f32 = jnp.float32
    x = jax.random.normal(ks[0], (BATCH, SEQ, D_MODEL), f32)
    pre_norm_gain = 1.0 + 0.05 * jax.random.normal(ks[1], (DEPTH, D_MODEL), f32)
    w_in = jax.random.normal(ks[2], (DEPTH, D_MODEL, IN_WIDTH), f32) * D_MODEL ** -0.5
    conv_w = jax.random.normal(ks[3], (DEPTH, CONV_WIDTH, CONV_CH), f32) * CONV_WIDTH ** -0.5
    conv_b = 0.01 * jax.random.normal(ks[4], (DEPTH, CONV_CH), f32)
    u = jax.random.uniform(ks[5], (DEPTH, SSD_HEADS), f32)
    dt0 = jnp.exp(u * (np.log(0.1) - np.log(0.001)) + np.log(0.001))
    dt_bias = dt0 + jnp.log(-jnp.expm1(-dt0))
    a_log = jnp.log(jax.random.uniform(ks[6], (DEPTH, SSD_HEADS), f32, 1.0, 16.0))
    d_skip = 1.0 + 0.1 * jax.random.normal(ks[7], (DEPTH, SSD_HEADS), f32)
    ssd_norm_gain = 1.0 + 0.05 * jax.random.normal(ks[8], (DEPTH, SSD_WIDTH), f32)
    w_branch_attn = jax.random.normal(ks[9], (DEPTH, ATTN_WIDTH, D_MODEL), f32) * ATTN_WIDTH ** -0.5
    w_branch_ssd = jax.random.normal(ks[10], (DEPTH, SSD_WIDTH, D_MODEL), f32) * SSD_WIDTH ** -0.5
    w_out = jax.random.normal(ks[11], (DEPTH, D_MODEL, D_MODEL), f32) * D_MODEL ** -0.5
    post_norm_gain = 1.0 + 0.05 * jax.random.normal(ks[12], (DEPTH, D_MODEL), f32)
    return {"x": x, "pre_norm_gain": pre_norm_gain, "w_in": w_in, "conv_w": conv_w,
            "conv_b": conv_b, "dt_bias": dt_bias, "a_log": a_log, "d_skip": d_skip,
            "ssd_norm_gain": ssd_norm_gain, "w_branch_attn": w_branch_attn,
            "w_branch_ssd": w_branch_ssd, "w_out": w_out, "post_norm_gain": post_norm_gain}


def reference(x, pre_norm_gain, w_in, conv_w, conv_b, dt_bias, a_log, d_skip,
              ssd_norm_gain, w_branch_attn, w_branch_ssd, w_out, post_norm_gain):
    b, s, _ = x.shape
    cos_a, sin_a = rope_tables(s, ATTN_HEAD_DIM)
    cos_i, sin_i = rope_tables(s, IDX_HEAD_DIM)
    split_points = [int(c) for c in np.cumsum(SPLIT_SIZES)[:-1]]
    idx_weight_scale = IDX_HEADS ** -0.5 * IDX_HEAD_DIM ** -0.5
    for layer in range(DEPTH):
        h = rms_norm(x, pre_norm_gain[layer])
        proj = h @ w_in[layer]
        (q, k, v, gate_a, qi, ki, wi, z, xs, bm, cm, dt_raw, merge_logits) = jnp.split(
            proj, split_points, axis=-1)

        q = apply_rope(q.reshape(b, s, ATTN_HEADS, ATTN_HEAD_DIM), cos_a, sin_a)
        k = apply_rope(k.reshape(b, s, ATTN_HEADS, ATTN_HEAD_DIM), cos_a, sin_a)
        v = v.reshape(b, s, ATTN_HEADS, ATTN_HEAD_DIM)
        qi = apply_rope(qi.reshape(b, s, IDX_HEADS, IDX_HEAD_DIM), cos_i, sin_i)
        ki = apply_rope(ki[:, :, None, :], cos_i, sin_i)[:, :, 0, :]
        attn = dsa_attention(q, k, v, qi, ki, wi * idx_weight_scale)
        y_a = attn.reshape(b, s, ATTN_WIDTH) * jax.nn.silu(gate_a)

        y_b = ssd_mixer(xs, z, bm, cm, dt_raw, conv_w[layer], conv_b[layer], dt_bias[layer],
                        a_log[layer], d_skip[layer], ssd_norm_gain[layer])

        gates = jax.nn.sigmoid(merge_logits.astype(jnp.float32)).reshape(b, s, N_BRANCHES, D_MODEL)
        merged = (gates[:, :, 0, :] * (y_a @ w_branch_attn[layer]).astype(jnp.float32)
                  + gates[:, :, 1, :] * (y_b @ w_branch_ssd[layer]).astype(jnp.float32))
        out = merged.astype(x.dtype) @ w_out[layer]
        x = x + rms_norm(out, post_norm_gain[layer])
    return x
```

```python
import functools

import numpy as np
import jax
import jax.numpy as jnp
from jax import lax
from jax.experimental import pallas as pl
from jax.experimental.pallas import tpu as pltpu

CHUNK = 64
ROPE_THETA = 10000.0
NORM_EPS = 1e-6
ATTN_HEAD_DIM = 128
IDX_HEADS = 32
IDX_HEAD_DIM = 64
TOPK_MAX = 256
SSD_HEAD_DIM = 64
SSD_GROUPS = 8
SSD_STATE = 128
CONV_WIDTH = 4

LANES = 128
VMEM_LIMIT_BYTES = 56 * 1024 * 1024

NEG_BIAS = -1e30
INT_MIN = -2147483648

_F32 = jnp.float32
_BF16 = jnp.bfloat16


def _params(*sem):
    return pltpu.CompilerParams(dimension_semantics=sem, vmem_limit_bytes=VMEM_LIMIT_BYTES)


def _rmsnorm_kernel(x_ref, g_ref, o_ref):
    x = x_ref[...]
    ms = jnp.mean(x * x, axis=-1, keepdims=True)
    o_ref[...] = (x * lax.rsqrt(ms + NORM_EPS) * g_ref[...]).astype(o_ref.dtype)


def _rmsnorm(x, gain, tm):
    s, d = x.shape
    return pl.pallas_call(
        _rmsnorm_kernel,
        grid=(s // tm,),
        in_specs=[pl.BlockSpec((tm, d), lambda i: (i, 0)),
                  pl.BlockSpec((1, d), lambda i: (0, 0))],
        out_specs=pl.BlockSpec((tm, d), lambda i: (i, 0)),
        out_shape=jax.ShapeDtypeStruct((s, d), _BF16),
        compiler_params=_params("parallel"),
        name="pre_rmsnorm",
    )(x, gain.reshape(1, d))


def _matmul_kernel(a_ref, b_ref, o_ref):
    o_ref[...] = jnp.dot(a_ref[...], b_ref[...],
                         preferred_element_type=_F32).astype(o_ref.dtype)


def _matmul(a, b, tm, tn, out_dtype, name):
    m, k = a.shape
    _, n = b.shape
    tm = min(tm, m)
    tn = min(tn, n)
    assert m % tm == 0 and n % tn == 0
    return pl.pallas_call(
        _matmul_kernel,
        grid=(m // tm, n // tn),
        in_specs=[pl.BlockSpec((tm, k), lambda i, j: (i, 0)),
                  pl.BlockSpec((k, tn), lambda i, j: (0, j))],
        out_specs=pl.BlockSpec((tm, tn), lambda i, j: (i, j)),
        out_shape=jax.ShapeDtypeStruct((m, n), out_dtype),
        compiler_params=_params("parallel", "arbitrary"),
        name=name,
    )(a, b)


def _rope_pairs64(x, c4, s4):
    lane = lax.broadcasted_iota(jnp.int32, x.shape, 1)
    first_half = (lane % IDX_HEAD_DIM) < (IDX_HEAD_DIM // 2)
    partner = jnp.where(first_half,
                        pltpu.roll(x, LANES - IDX_HEAD_DIM // 2, 1),
                        pltpu.roll(x, IDX_HEAD_DIM // 2, 1))
    return x * c4 + partner * s4


def _prep_kernel(q_ref, k_ref, v_ref, qi_ref, tail_ref, ca_ref, sa_ref, ci_ref, si_ref,
                 qo_ref, ko_ref, vo_ref, qio_ref, kio_ref, wo_ref, *, heads, w_scale):
    ca = ca_ref[...]
    sa = sa_ref[...]
    for h in range(heads):
        sl = slice(h * LANES, (h + 1) * LANES)
        xq = q_ref[:, sl]
        qo_ref[:, sl] = (xq * ca + pltpu.roll(xq, LANES // 2, 1) * sa).astype(qo_ref.dtype)
        xk = k_ref[:, sl]
        ko_ref[:, sl] = (xk * ca + pltpu.roll(xk, LANES // 2, 1) * sa).astype(ko_ref.dtype)
    vo_ref[...] = v_ref[...].astype(vo_ref.dtype)
    ci = ci_ref[...]
    si = si_ref[...]
    for g in range(IDX_HEADS // 2):
        r = _rope_pairs64(qi_ref[:, g * LANES:(g + 1) * LANES], ci, si).astype(qio_ref.dtype)
        qio_ref[0, 2 * g] = r[:, :IDX_HEAD_DIM]
        qio_ref[0, 2 * g + 1] = r[:, IDX_HEAD_DIM:]
    t = tail_ref[...]
    kio_ref[...] = _rope_pairs64(t, ci, si)[:, :IDX_HEAD_DIM].astype(kio_ref.dtype)
    wo_ref[...] = t[:, IDX_HEAD_DIM:IDX_HEAD_DIM + IDX_HEADS] * w_scale


def _prep(proj, tail, tabs, aw, tq):
    s = proj.shape[0]
    heads = aw // ATTN_HEAD_DIM
    qi_w = IDX_HEADS * IDX_HEAD_DIM
    qi_blk = (4 * aw) // qi_w
    assert (4 * aw) % qi_w == 0
    nb = s // tq
    row = lambda i: (i, 0)
    kern = functools.partial(_prep_kernel, heads=heads,
                             w_scale=float(IDX_HEADS ** -0.5 * IDX_HEAD_DIM ** -0.5))
    return pl.pallas_call(
        kern,
        grid=(nb,),
        in_specs=[pl.BlockSpec((tq, aw), lambda i: (i, 0)),
                  pl.BlockSpec((tq, aw), lambda i: (i, 1)),
                  pl.BlockSpec((tq, aw), lambda i: (i, 2)),
                  pl.BlockSpec((tq, qi_w), lambda i: (i, qi_blk)),
                  pl.BlockSpec((tq, LANES), row),
                  pl.BlockSpec((tq, LANES), row),
                  pl.BlockSpec((tq, LANES), row),
                  pl.BlockSpec((tq, LANES), row),
                  pl.BlockSpec((tq, LANES), row)],
        out_specs=[pl.BlockSpec((tq, aw), row),
                   pl.BlockSpec((tq, aw), row),
                   pl.BlockSpec((tq, aw), row),
                   pl.BlockSpec((1, IDX_HEADS, tq, IDX_HEAD_DIM), lambda i: (i, 0, 0, 0)),
                   pl.BlockSpec((tq, IDX_HEAD_DIM), row),
                   pl.BlockSpec((tq, IDX_HEADS), row)],
        out_shape=[jax.ShapeDtypeStruct((s, aw), _BF16),
                   jax.ShapeDtypeStruct((s, aw), _BF16),
                   jax.ShapeDtypeStruct((s, aw), _BF16),
                   jax.ShapeDtypeStruct((nb, IDX_HEADS, tq, IDX_HEAD_DIM), _BF16),
                   jax.ShapeDtypeStruct((s, IDX_HEAD_DIM), _BF16),
                   jax.ShapeDtypeStruct((s, IDX_HEADS), _F32)],
        compiler_params=_params("parallel"),
        name="rope_prep",
    )(proj, proj, proj, proj, tail, *tabs)


def _index_kernel(qi_ref, ki_ref, w_ref, bias_ref, wb_ref, key_ref, *, tq, kb, nkb_total, topk,
                  head_group):
    i = pl.program_id(0)
    nkb = ((i + 1) * tq + kb - 1) // kb
    ncol = kb // LANES

    w = w_ref[...]
    for h in range(IDX_HEADS):
        wb_ref[h * tq:(h + 1) * tq, :] = jnp.broadcast_to(w[:, h:h + 1], (tq, LANES))

    row = lax.broadcasted_iota(jnp.int32, (tq, LANES), 0)
    lane = lax.broadcasted_iota(jnp.int32, (tq, LANES), 1)
    vis_end = i * tq + (row // CHUNK + 1) * CHUNK

    def score_body(j, carry):
        kj = ki_ref[j]
        acc = [jnp.zeros((tq, LANES), _F32) for _ in range(ncol)]
        for hg in range(IDX_HEADS // head_group):
            q = qi_ref[0, hg * head_group:(hg + 1) * head_group].reshape(
                head_group * tq, IDX_HEAD_DIM)
            logit = lax.dot_general(q, kj, (((1,), (1,)), ((), ())),
                                    preferred_element_type=_F32)
            wb = wb_ref[hg * head_group * tq:(hg + 1) * head_group * tq, :]
            for c in range(ncol):
                r = jnp.maximum(logit[:, c * LANES:(c + 1) * LANES], 0.0) * wb
                acc[c] = acc[c] + jnp.sum(r.reshape(head_group, tq, LANES), axis=0)
        for c in range(ncol):
            bits = pltpu.bitcast(acc[c], jnp.int32)
            key = bits ^ ((bits >> 31) & jnp.int32(0x7FFFFFFF))
            pos = j * kb + c * LANES + lane
            key_ref[j, :, c * LANES:(c + 1) * LANES] = jnp.where(pos < vis_end, key, INT_MIN)
        return carry

    lax.fori_loop(0, nkb, score_body, 0)

    def bit_body(it, thr):
        cand = thr + lax.shift_left(jnp.int32(1), 31 - it)

        def cnt_body(j, cnt):
            for c in range(ncol):
                kk = key_ref[j, :, c * LANES:(c + 1) * LANES]
                cnt = cnt + jnp.where(kk >= cand, 1.0, 0.0)
            return cnt

        cnt = lax.fori_loop(0, nkb, cnt_body, jnp.zeros((tq, LANES), _F32))
        tot = jnp.sum(cnt, axis=1, keepdims=True)
        return jnp.where(tot >= float(topk), cand, thr)

    thr = lax.fori_loop(0, 32, bit_body, jnp.full((tq, LANES), INT_MIN, jnp.int32))
    thr = jnp.maximum(thr, INT_MIN + 1)

    def out_body(j, carry):
        for c in range(ncol):
            kk = key_ref[j, :, c * LANES:(c + 1) * LANES]
            bias_ref[0, j, :, c * LANES:(c + 1) * LANES] = jnp.where(kk >= thr, 0.0, NEG_BIAS)
        return carry

    lax.fori_loop(0, nkb, out_body, 0)

    def fill_body(j, carry):
        bias_ref[0, j] = jnp.full((tq, kb), NEG_BIAS, _F32)
        return carry

    lax.fori_loop(nkb, nkb_total, fill_body, 0)


def _index_bias(qi_hm, ki_rot, w, tq, kb, topk):
    nb = qi_hm.shape[0]
    s = ki_rot.shape[0]
    nkb_total = s // kb
    ki3 = ki_rot.reshape(nkb_total, kb, IDX_HEAD_DIM)
    kern = functools.partial(_index_kernel, tq=tq, kb=kb, nkb_total=nkb_total, topk=topk,
                             head_group=8)
    return pl.pallas_call(
        kern,
        grid=(nb,),
        in_specs=[pl.BlockSpec((1, IDX_HEADS, tq, IDX_HEAD_DIM), lambda i: (i, 0, 0, 0)),
                  pl.BlockSpec((nkb_total, kb, IDX_HEAD_DIM), lambda i: (0, 0, 0)),
                  pl.BlockSpec((tq, IDX_HEADS), lambda i: (i, 0))],
        out_specs=pl.BlockSpec((1, nkb_total, tq, kb), lambda i: (i, 0, 0, 0)),
        out_shape=jax.ShapeDtypeStruct((nb, nkb_total, tq, kb), _F32),
        scratch_shapes=[pltpu.VMEM((IDX_HEADS * tq, LANES), _F32),
                        pltpu.VMEM((nkb_total, tq, kb), jnp.int32)],
        compiler_params=_params("parallel"),
        name="index_topk_bias",
    )(qi_hm, ki3, w)


def _attn_kernel(q_ref, k_ref, v_ref, b_ref, g_ref, o_ref, m_ref, l_ref, acc_ref, *,
                 heads, tq, kb, scale):
    i = pl.program_id(0)
    j = pl.program_id(1)
    last = ((i + 1) * tq - 1) // kb
    nrep = kb // LANES

    @pl.when(j == 0)
    def _():
        m_ref[...] = jnp.full(m_ref.shape, NEG_BIAS, _F32)
        l_ref[...] = jnp.zeros(l_ref.shape, _F32)
        acc_ref[...] = jnp.zeros(acc_ref.shape, _F32)

    @pl.when(j <= last)
    def _():
        bias = b_ref[...].reshape(tq, kb)
        for h in range(heads):
            sl = slice(h * ATTN_HEAD_DIM, (h + 1) * ATTN_HEAD_DIM)
            s = lax.dot_general(q_ref[:, sl], k_ref[:, sl], (((1,), (1,)), ((), ())),
                                preferred_element_type=_F32)
            s = s * scale + bias
            m_prev = m_ref[h]
            m_next = jnp.maximum(m_prev, jnp.max(s, axis=1, keepdims=True))
            alpha = jnp.exp(m_prev - m_next)
            p = jnp.exp(s - jnp.concatenate([m_next] * nrep, axis=1))
            l_ref[h] = alpha * l_ref[h] + jnp.sum(p, axis=1, keepdims=True)
            m_ref[h] = m_next
            pv = jnp.dot(p.astype(_BF16), v_ref[:, sl], preferred_element_type=_F32)
            acc_ref[:, sl] = alpha * acc_ref[:, sl] + pv

    @pl.when(j == last)
    def _():
        for h in range(heads):
            sl = slice(h * ATTN_HEAD_DIM, (h + 1) * ATTN_HEAD_DIM)
            g = g_ref[:, sl]
            o_ref[:, sl] = (acc_ref[:, sl] / l_ref[h] * (g * jax.nn.sigmoid(g))).astype(o_ref.dtype)


def _attention(q, k, v, bias, proj, aw, tq, kb, tq_idx):
    s = q.shape[0]
    heads = aw // ATTN_HEAD_DIM
    nq = s // tq
    nkb = s // kb
    rq = tq // tq_idx
    kmap = lambda i, j: (jnp.minimum(j, ((i + 1) * tq - 1) // kb), 0)
    kern = functools.partial(_attn_kernel, heads=heads, tq=tq, kb=kb,
                             scale=float(ATTN_HEAD_DIM ** -0.5))
    return pl.pallas_call(
        kern,
        grid=(nq, nkb),
        in_specs=[pl.BlockSpec((tq, aw), lambda i, j: (i, 0)),
                  pl.BlockSpec((kb, aw), kmap),
                  pl.BlockSpec((kb, aw), kmap),
                  pl.BlockSpec((rq, 1, tq_idx, kb),
                               lambda i, j: (i, jnp.minimum(j, ((i + 1) * tq - 1) // kb), 0, 0)),
                  pl.BlockSpec((tq, aw), lambda i, j: (i, 3))],
        out_specs=pl.BlockSpec((tq, aw), lambda i, j: (i, 0)),
        out_shape=jax.ShapeDtypeStruct((s, aw), _BF16),
        scratch_shapes=[pltpu.VMEM((heads, tq, LANES), _F32),
                        pltpu.VMEM((heads, tq, LANES), _F32),
                        pltpu.VMEM((tq, aw), _F32)],
        compiler_params=_params("parallel", "arbitrary"),
        name="masked_attention",
    )(q, k, v, bias, proj)


def _softplus(x):
    return jnp.maximum(x, 0.0) + jnp.log1p(jnp.exp(-jnp.abs(x)))


def _silu(x):
    return x * jax.nn.sigmoid(x)


def _dot_f32(a, b):
    return jnp.dot(a, b, preferred_element_type=_F32, precision=lax.Precision.HIGHEST)


def _ssd_kernel(z_ref, xs_ref, bm_ref, cm_ref, tail_ref, cw_ref, cb_ref, dtb_ref, alog_ref,
                dsk_ref, gain_ref, eh_ref, er_ref, tri_ref, o_ref,
                xs_ext, bm_ext, cm_ext, st_ref, *, blk, sw, gn, heads_per_group):
    i = pl.program_id(0)
    gw = heads_per_group * SSD_HEAD_DIM
    halo = 8

    @pl.when(i == 0)
    def _():
        xs_ext[0:halo, :] = jnp.zeros((halo, sw), _F32)
        bm_ext[0:halo, :] = jnp.zeros((halo, gn), _F32)
        cm_ext[0:halo, :] = jnp.zeros((halo, gn), _F32)
        st_ref[...] = jnp.zeros(st_ref.shape, _F32)

    def conv(ext, src_ref, off, width):
        ext[halo:halo + blk, :] = src_ref[...]
        acc = cb_ref[:, off:off + width]
        for t in range(CONV_WIDTH):
            start = halo - (CONV_WIDTH - 1) + t
            acc = acc + ext[start:start + blk, :] * cw_ref[t:t + 1, off:off + width]
        ext[0:halo, :] = ext[blk:blk + halo, :]
        return _silu(acc)

    xs_c = conv(xs_ext, xs_ref, 0, sw)
    bm_c = conv(bm_ext, bm_ref, sw, gn)
    cm_c = conv(cm_ext, cm_ref, sw + gn, gn)

    dt = _softplus(tail_ref[...] + dtb_ref[...])
    dta = dt * (-jnp.exp(alog_ref[...]))
    a_cum = _dot_f32(tri_ref[...], dta)
    a_cum_t = a_cum.T
    dt_e = _dot_f32(dt, eh_ref[...])
    acum_e = _dot_f32(a_cum, eh_ref[...])
    acum_rep = _dot_f32(a_cum, er_ref[...])
    a_last_e = acum_e[blk - 1:blk, :]

    xdt = xs_c * dt_e
    xw = (xdt * jnp.exp(a_last_e - acum_e)).astype(_BF16)
    xdt16 = xdt.astype(_BF16)
    exp_acum = jnp.exp(acum_e)
    bm16 = bm_c.astype(_BF16)
    cm16 = cm_c.astype(_BF16)

    rowi = lax.broadcasted_iota(jnp.int32, (blk, blk), 0)
    coli = lax.broadcasted_iota(jnp.int32, (blk, blk), 1)
    causal = rowi >= coli
    lane_g = lax.broadcasted_iota(jnp.int32, (blk, gw), 1)

    for g in range(SSD_GROUPS):
        bg = bm16[:, g * SSD_STATE:(g + 1) * SSD_STATE]
        cg = cm16[:, g * SSD_STATE:(g + 1) * SSD_STATE]
        cbm = lax.dot_general(cg, bg, (((1,), (1,)), ((), ())), preferred_element_type=_F32)
        x_g = xdt16[:, g * gw:(g + 1) * gw]
        lhs = []
        rhs = []
        for r in range(heads_per_group):
            h = g * heads_per_group + r
            seg = acum_rep[:, h * blk:(h + 1) * blk] - a_cum_t[LANES - 32 + h:LANES - 31 + h, :]
            decay = jnp.where(causal, jnp.exp(seg), 0.0)
            lhs.append((cbm * decay).astype(_BF16))
            in_head = (lane_g >= r * SSD_HEAD_DIM) & (lane_g < (r + 1) * SSD_HEAD_DIM)
            rhs.append(jnp.where(in_head, x_g, jnp.zeros_like(x_g)))
        y_diag = jnp.dot(jnp.concatenate(lhs, axis=1), jnp.concatenate(rhs, axis=0),
                         preferred_element_type=_F32)
        st = st_ref[g]
        y_off = jnp.dot(cg, st.astype(_BF16), preferred_element_type=_F32)
        sl = slice(g * gw, (g + 1) * gw)
        y = y_diag + y_off * exp_acum[:, sl] + xs_c[:, sl] * dsk_ref[:, sl]
        yg = y * _silu(z_ref[:, sl])
        ms = jnp.mean(yg * yg, axis=-1, keepdims=True)
        o_ref[:, sl] = (yg * lax.rsqrt(ms + NORM_EPS) * gain_ref[:, sl]).astype(o_ref.dtype)
        bg_t = bm_c[:, g * SSD_STATE:(g + 1) * SSD_STATE].T.astype(_BF16)
        upd = jnp.dot(bg_t, xw[:, sl], preferred_element_type=_F32)
        st_ref[g] = st * jnp.exp(a_last_e[:, sl]) + upd


def _ssd(proj, tail, conv_w, conv_b, dt_bias, a_log, d_skip, norm_gain, sw, col_z, blk):
    s = proj.shape[0]
    heads = sw // SSD_HEAD_DIM
    hpg = heads // SSD_GROUPS
    gn = SSD_GROUPS * SSD_STATE
    gw = hpg * SSD_HEAD_DIM
    assert blk == LANES and heads <= 32 and sw // SSD_GROUPS == gw
    conv_ch = sw + 2 * gn
    pad = LANES - 32
    place = lambda v: jnp.zeros((1, LANES), _F32).at[0, pad:pad + heads].set(v)
    dtb = place(dt_bias)
    alog = place(a_log)
    dsk = jnp.repeat(d_skip, SSD_HEAD_DIM).reshape(1, sw)
    eh = np.zeros((LANES, sw), np.float32)
    er = np.zeros((LANES, heads * blk), np.float32)
    for h in range(heads):
        eh[pad + h, h * SSD_HEAD_DIM:(h + 1) * SSD_HEAD_DIM] = 1.0
        er[pad + h, h * blk:(h + 1) * blk] = 1.0
    tri = np.tril(np.ones((blk, blk), np.float32))
    assert col_z % sw == 0 and (col_z + 2 * sw) % gn == 0
    z_blk = col_z // sw
    b_blk = (col_z + 2 * sw) // gn
    full = lambda shape: pl.BlockSpec(shape, lambda i: (0, 0))
    kern = functools.partial(_ssd_kernel, blk=blk, sw=sw, gn=gn, heads_per_group=hpg)
    return pl.pallas_call(
        kern,
        grid=(s // blk,),
        in_specs=[pl.BlockSpec((blk, sw), lambda i: (i, z_blk)),
                  pl.BlockSpec((blk, sw), lambda i: (i, z_blk + 1)),
                  pl.BlockSpec((blk, gn), lambda i: (i, b_blk)),
                  pl.BlockSpec((blk, gn), lambda i: (i, b_blk + 1)),
                  pl.BlockSpec((blk, LANES), lambda i: (i, 0)),
                  full((CONV_WIDTH, conv_ch)), full((1, conv_ch)),
                  full((1, LANES)), full((1, LANES)), full((1, sw)), full((1, sw)),
                  full((LANES, sw)), full((LANES, heads * blk)), full((blk, blk))],
        out_specs=pl.BlockSpec((blk, sw), lambda i: (i, 0)),
        out_shape=jax.ShapeDtypeStruct((s, sw), _BF16),
        scratch_shapes=[pltpu.VMEM((blk + 8, sw), _F32),
                        pltpu.VMEM((blk + 8, gn), _F32),
                        pltpu.VMEM((blk + 8, gn), _F32),
                        pltpu.VMEM((SSD_GROUPS, SSD_STATE, gw), _F32)],
        compiler_params=_params("arbitrary"),
        name="ssd_mixer",
    )(proj, proj, proj, proj, tail, conv_w, conv_b.reshape(1, conv_ch), dtb, alog, dsk,
      norm_gain.reshape(1, sw), jnp.asarray(eh), jnp.asarray(er), jnp.asarray(tri))


def _merge_kernel(ya_ref, yb_ref, wa_ref, wb_ref, la_ref, lb_ref, o_ref):
    pa = jnp.dot(ya_ref[...], wa_ref[...], preferred_element_type=_F32)
    pb = jnp.dot(yb_ref[...], wb_ref[...], preferred_element_type=_F32)
    o_ref[...] = (jax.nn.sigmoid(la_ref[...]) * pa
                  + jax.nn.sigmoid(lb_ref[...]) * pb).astype(o_ref.dtype)


def _merge(ya, yb, wa, wb, proj, col_merge, tm, tn):
    s, aw = ya.shape
    d = wa.shape[1]
    tm = min(tm, s)
    tn = min(tn, d)
    assert col_merge % tn == 0 and d % tn == 0
    ca = col_merge // tn
    cb = (col_merge + d) // tn
    return pl.pallas_call(
        _merge_kernel,
        grid=(s // tm, d // tn),
        in_specs=[pl.BlockSpec((tm, aw), lambda i, j: (i, 0)),
                  pl.BlockSpec((tm, yb.shape[1]), lambda i, j: (i, 0)),
                  pl.BlockSpec((aw, tn), lambda i, j: (0, j)),
                  pl.BlockSpec((yb.shape[1], tn), lambda i, j: (0, j)),
                  pl.BlockSpec((tm, tn), lambda i, j: (i, ca + j)),
                  pl.BlockSpec((tm, tn), lambda i, j: (i, cb + j))],
        out_specs=pl.BlockSpec((tm, tn), lambda i, j: (i, j)),
        out_shape=jax.ShapeDtypeStruct((s, d), _BF16),
        compiler_params=_params("parallel", "arbitrary"),
        name="gated_merge",
    )(ya, yb, wa, wb, proj, proj)


def _postnorm_kernel(x_ref, o_in_ref, g_ref, o_ref):
    o = o_in_ref[...]
    ms = jnp.mean(o * o, axis=-1, keepdims=True)
    o_ref[...] = x_ref[...] + o * lax.rsqrt(ms + NORM_EPS) * g_ref[...]


def _postnorm(x, out, gain, tm):
    s, d = x.shape
    blk = pl.BlockSpec((tm, d), lambda i: (i, 0))
    return pl.pallas_call(
        _postnorm_kernel,
        grid=(s // tm,),
        in_specs=[blk, blk, pl.BlockSpec((1, d), lambda i: (0, 0))],
        out_specs=blk,
        out_shape=jax.ShapeDtypeStruct((s, d), _F32),
        compiler_params=_params("parallel"),
        name="post_rmsnorm_residual",
    )(x, out, gain.reshape(1, d))


def _rope_tables(seq_len):
    def tab(dim):
        inv_freq = 1.0 / (ROPE_THETA ** (jnp.arange(0, dim, 2, dtype=_F32) / dim))
        ang = jnp.arange(seq_len, dtype=_F32)[:, None] * inv_freq[None, :]
        return jnp.cos(ang), jnp.sin(ang)
    ca, sa = tab(ATTN_HEAD_DIM)
    ci, si = tab(IDX_HEAD_DIM)
    return (jnp.concatenate([ca, ca], axis=1), jnp.concatenate([-sa, sa], axis=1),
            jnp.concatenate([ci, ci, ci, ci], axis=1), jnp.concatenate([-si, si, -si, si], axis=1))


def _layer(x2, pre_gain, w_in, conv_w, conv_b, dt_bias, a_log, d_skip, ssd_gain,
           w_a, w_b, w_out, post_gain, tabs):
    s, d = x2.shape
    aw = d // 2
    sw = d // 2
    gn = SSD_GROUPS * SSD_STATE
    ssd_heads = sw // SSD_HEAD_DIM
    qi_w = IDX_HEADS * IDX_HEAD_DIM
    sizes = (aw, aw, aw, aw, qi_w, IDX_HEAD_DIM, IDX_HEADS, sw, sw, gn, gn, ssd_heads, 2 * d)
    off = np.concatenate([[0], np.cumsum(sizes)])
    seg = lambda n: w_in[:, int(off[n]):int(off[n + 1])]
    assert IDX_HEAD_DIM + IDX_HEADS + ssd_heads <= LANES
    w_main = jnp.concatenate([seg(0), seg(1), seg(2), seg(3), seg(4), seg(7), seg(8), seg(9),
                              seg(10), seg(12)], axis=1).astype(_BF16)
    w_tail = jnp.concatenate(
        [seg(5), seg(6), seg(11), jnp.zeros((d, 32 - ssd_heads), w_in.dtype)],
        axis=1).astype(_BF16)
    col_z = 4 * aw + qi_w
    col_merge = col_z + 2 * sw + 2 * gn

    h = _rmsnorm(x2, pre_gain, tm=256)
    proj = _matmul(h, w_main, 1024, 1024, _F32, "in_proj")
    tail = _matmul(h, w_tail, 1024, LANES, _F32, "in_proj_tail")

    tq_idx = 128
    q, k, v, qi_hm, ki_rot, w_idx = _prep(proj, tail, tabs, aw, tq_idx)
    topk = min(TOPK_MAX, s // 4)
    kb = min(512, s)
    bias = _index_bias(qi_hm, ki_rot, w_idx, tq_idx, kb, topk)
    y_a = _attention(q, k, v, bias, proj, aw, min(256, s), kb, tq_idx)

    y_b = _ssd(proj, tail, conv_w, conv_b, dt_bias, a_log, d_skip, ssd_gain, sw, col_z, LANES)

    merged = _merge(y_a, y_b, w_a.astype(_BF16), w_b.astype(_BF16), proj, col_merge, 512, 1024)
    out = _matmul(merged, w_out.astype(_BF16), 1024, 1024, _F32, "out_proj")
    return _postnorm(x2, out, post_gain, tm=256)


def kernel(x, pre_norm_gain, w_in, conv_w, conv_b, dt_bias, a_log, d_skip, ssd_norm_gain,
           w_branch_attn, w_branch_ssd, w_out, post_norm_gain):
    b, s, d = x.shape
    tabs = _rope_tables(s)
    outs = []
    for bi in range(b):
        xb = x[bi]
        for layer in range(pre_norm_gain.shape[0]):
            xb = _layer(xb, pre_norm_gain[layer], w_in[layer], conv_w[layer], conv_b[layer],
                        dt_bias[layer], a_log[layer], d_skip[layer], ssd_norm_gain[layer],
                        w_branch_attn[layer], w_branch_ssd[layer], w_out[layer],
                        post_norm_gain[layer], tabs)
        outs.append(xb)
    return jnp.stack(outs, axis=0)
```

```python
import functools

import numpy as np
import jax
import jax.numpy as jnp
from jax import lax
from jax.experimental import pallas as pl
from jax.experimental.pallas import tpu as pltpu

CHUNK = 64
ROPE_THETA = 10000.0
NORM_EPS = 1e-6
ATTN_HEAD_DIM = 128
IDX_HEADS = 32
IDX_HEAD_DIM = 64
TOPK_MAX = 256
SSD_HEAD_DIM = 64
SSD_GROUPS = 8
SSD_STATE = 128
CONV_WIDTH = 4

LANES = 128
VMEM_LIMIT_BYTES = 56 * 1024 * 1024

NEG_BIAS = -1e30

_F32 = jnp.float32
_BF16 = jnp.bfloat16


def _params(*sem):
    return pltpu.CompilerParams(dimension_semantics=sem, vmem_limit_bytes=VMEM_LIMIT_BYTES)


def _rmsnorm_kernel(x_ref, g_ref, o_ref):
    x = x_ref[...]
    ms = jnp.mean(x * x, axis=-1, keepdims=True)
    o_ref[...] = (x * lax.rsqrt(ms + NORM_EPS) * g_ref[...]).astype(o_ref.dtype)


def _rmsnorm(x, gain, tm):
    s, d = x.shape
    return pl.pallas_call(
        _rmsnorm_kernel,
        grid=(s // tm,),
        in_specs=[pl.BlockSpec((tm, d), lambda i: (i, 0)),
                  pl.BlockSpec((1, d), lambda i: (0, 0))],
        out_specs=pl.BlockSpec((tm, d), lambda i: (i, 0)),
        out_shape=jax.ShapeDtypeStruct((s, d), _BF16),
        compiler_params=_params("parallel"),
        name="pre_rmsnorm",
    )(x, gain.reshape(1, d))


def _matmul_kernel(a_ref, b_ref, o_ref):
    o_ref[...] = jnp.dot(a_ref[...], b_ref[...],
                         preferred_element_type=_F32).astype(o_ref.dtype)


def _matmul(a, b, tm, tn, out_dtype, name):
    m, k = a.shape
    _, n = b.shape
    tm = min(tm, m)
    tn = min(tn, n)
    assert m % tm == 0 and n % tn == 0
    return pl.pallas_call(
        _matmul_kernel,
        grid=(m // tm, n // tn),
        in_specs=[pl.BlockSpec((tm, k), lambda i, j: (i, 0)),
                  pl.BlockSpec((k, tn), lambda i, j: (0, j))],
        out_specs=pl.BlockSpec((tm, tn), lambda i, j: (i, j)),
        out_shape=jax.ShapeDtypeStruct((m, n), out_dtype),
        compiler_params=_params("parallel", "arbitrary"),
        name=name,
    )(a, b)


def _rope_pairs64(x, c4, s4):
    lane = lax.broadcasted_iota(jnp.int32, x.shape, 1)
    first_half = (lane % IDX_HEAD_DIM) < (IDX_HEAD_DIM // 2)
    partner = jnp.where(first_half,
                        pltpu.roll(x, LANES - IDX_HEAD_DIM // 2, 1),
                        pltpu.roll(x, IDX_HEAD_DIM // 2, 1))
    return x * c4 + partner * s4


def _prep_kernel(q_ref, k_ref, v_ref, qi_ref, tail_ref, ca_ref, sa_ref, ci_ref, si_ref,
                 qo_ref, ko_ref, vo_ref, qio_ref, kio_ref, wo_ref, *, heads, w_scale):
    ca = ca_ref[...]
    sa = sa_ref[...]
    for h in range(heads):
        sl = slice(h * LANES, (h + 1) * LANES)
        xq = q_ref[:, sl]
        qo_ref[:, sl] = (xq * ca + pltpu.roll(xq, LANES // 2, 1) * sa).astype(qo_ref.dtype)
        xk = k_ref[:, sl]
        ko_ref[:, sl] = (xk * ca + pltpu.roll(xk, LANES // 2, 1) * sa).astype(ko_ref.dtype)
    ones = jnp.ones((v_ref.shape[0], LANES), vo_ref.dtype)
    for h in range(heads):
        vo_ref[:, 2 * h * LANES:(2 * h + 1) * LANES] = v_ref[:, h * LANES:(h + 1) * LANES].astype(
            vo_ref.dtype)
        vo_ref[:, (2 * h + 1) * LANES:(2 * h + 2) * LANES] = ones
    ci = ci_ref[...]
    si = si_ref[...]
    for g in range(IDX_HEADS // 2):
        r = _rope_pairs64(qi_ref[:, g * LANES:(g + 1) * LANES], ci, si).astype(qio_ref.dtype)
        qio_ref[0, 2 * g] = r[:, :IDX_HEAD_DIM]
        qio_ref[0, 2 * g + 1] = r[:, IDX_HEAD_DIM:]
    t = tail_ref[...]
    kio_ref[...] = _rope_pairs64(t, ci, si)[:, :IDX_HEAD_DIM].astype(kio_ref.dtype)
    wo_ref[...] = t[:, IDX_HEAD_DIM:IDX_HEAD_DIM + IDX_HEADS] * w_scale


def _prep(proj, tail, tabs, aw, tq):
    s = proj.shape[0]
    heads = aw // ATTN_HEAD_DIM
    qi_w = IDX_HEADS * IDX_HEAD_DIM
    qi_blk = (4 * aw) // qi_w
    assert (4 * aw) % qi_w == 0
    nb = s // tq
    row = lambda i: (i, 0)
    kern = functools.partial(_prep_kernel, heads=heads,
                             w_scale=float(IDX_HEADS ** -0.5 * IDX_HEAD_DIM ** -0.5))
    return pl.pallas_call(
        kern,
        grid=(nb,),
        in_specs=[pl.BlockSpec((tq, aw), lambda i: (i, 0)),
                  pl.BlockSpec((tq, aw), lambda i: (i, 1)),
                  pl.BlockSpec((tq, aw), lambda i: (i, 2)),
                  pl.BlockSpec((tq, qi_w), lambda i: (i, qi_blk)),
                  pl.BlockSpec((tq, LANES), row),
                  pl.BlockSpec((tq, LANES), row),
                  pl.BlockSpec((tq, LANES), row),
                  pl.BlockSpec((tq, LANES), row),
                  pl.BlockSpec((tq, LANES), row)],
        out_specs=[pl.BlockSpec((tq, aw), row),
                   pl.BlockSpec((tq, aw), row),
                   pl.BlockSpec((tq, 2 * aw), row),
                   pl.BlockSpec((1, IDX_HEADS, tq, IDX_HEAD_DIM), lambda i: (i, 0, 0, 0)),
                   pl.BlockSpec((tq, IDX_HEAD_DIM), row),
                   pl.BlockSpec((tq, IDX_HEADS), row)],
        out_shape=[jax.ShapeDtypeStruct((s, aw), _BF16),
                   jax.ShapeDtypeStruct((s, aw), _BF16),
                   jax.ShapeDtypeStruct((s, 2 * aw), _BF16),
                   jax.ShapeDtypeStruct((nb, IDX_HEADS, tq, IDX_HEAD_DIM), _BF16),
                   jax.ShapeDtypeStruct((s, IDX_HEAD_DIM), _BF16),
                   jax.ShapeDtypeStruct((s, IDX_HEADS), _F32)],
        compiler_params=_params("parallel"),
        name="rope_prep",
    )(proj, proj, proj, proj, tail, *tabs)


def _index_kernel(qi_ref, ki_ref, w_ref, bias_ref, wb_ref, score_ref, *, tq, kb, nkb_total, topk,
                  head_group):
    i = pl.program_id(0)
    nkb = ((i + 1) * tq + kb - 1) // kb
    ncol = kb // LANES

    w = w_ref[...]
    for h in range(IDX_HEADS):
        wb_ref[h * tq:(h + 1) * tq, :] = jnp.broadcast_to(w[:, h:h + 1], (tq, LANES))

    row = lax.broadcasted_iota(jnp.int32, (tq, LANES), 0)
    lane = lax.broadcasted_iota(jnp.int32, (tq, LANES), 1)
    vis_end = i * tq + (row // CHUNK + 1) * CHUNK

    def score_body(j, carry):
        rmax, rmin = carry
        kj = ki_ref[j]
        acc = [jnp.zeros((tq, LANES), _F32) for _ in range(ncol)]
        for hg in range(IDX_HEADS // head_group):
            q = qi_ref[0, hg * head_group:(hg + 1) * head_group].reshape(
                head_group * tq, IDX_HEAD_DIM)
            logit = lax.dot_general(q, kj, (((1,), (1,)), ((), ())),
                                    preferred_element_type=_F32)
            wb = wb_ref[hg * head_group * tq:(hg + 1) * head_group * tq, :]
            for c in range(ncol):
                r = jnp.maximum(logit[:, c * LANES:(c + 1) * LANES], 0.0) * wb
                acc[c] = acc[c] + jnp.sum(r.reshape(head_group, tq, LANES), axis=0)
        for c in range(ncol):
            admissible = (j * kb + c * LANES + lane) < vis_end
            sc = jnp.where(admissible, acc[c], -jnp.inf)
            score_ref[j, :, c * LANES:(c + 1) * LANES] = sc
            rmax = jnp.maximum(rmax, sc)
            rmin = jnp.minimum(rmin, jnp.where(admissible, acc[c], jnp.inf))
        return rmax, rmin

    rmax, rmin = lax.fori_loop(
        0, nkb, score_body,
        (jnp.full((tq, LANES), -jnp.inf, _F32), jnp.full((tq, LANES), jnp.inf, _F32)))

    kf = float(topk)
    n_adm = vis_end[:, :1].astype(_F32)
    lo0 = jnp.min(rmin, axis=1, keepdims=True)
    hi0 = jnp.max(rmax, axis=1, keepdims=True)
    steps_per_check = 2
    max_checks = 160

    def bisect_step(lo, hi, clo, settled):
        mid = 0.5 * lo + 0.5 * hi
        mid_b = jnp.broadcast_to(mid, (tq, LANES))

        def cnt_body(j, cnt):
            for c in range(ncol):
                sc = score_ref[j, :, c * LANES:(c + 1) * LANES]
                cnt = cnt + jnp.where(sc >= mid_b, 1.0, 0.0)
            return cnt

        cnt = lax.fori_loop(0, nkb, cnt_body, jnp.zeros((tq, LANES), _F32))
        tot = jnp.sum(cnt, axis=1, keepdims=True)
        accept = tot >= kf
        stalled = jnp.logical_or(mid <= lo, mid >= hi)
        lo = jnp.where(accept, mid, lo)
        clo = jnp.where(accept, tot, clo)
        hi = jnp.where(accept, hi, mid)
        settled = jnp.maximum(settled, jnp.where(jnp.logical_or(clo == kf, stalled), 1.0, 0.0))
        return lo, hi, clo, settled

    def check_cond(state):
        n, _, _, _, _, pending = state
        return jnp.logical_and(n < max_checks, pending > 0.0)

    def check_body(state):
        n, lo, hi, clo, settled, _ = state
        for _ in range(steps_per_check):
            lo, hi, clo, settled = bisect_step(lo, hi, clo, settled)
        return n + 1, lo, hi, clo, settled, jnp.max(1.0 - settled)

    settled0 = jnp.where(n_adm <= kf, 1.0, 0.0)
    _, lo, _, _, _, _ = lax.while_loop(
        check_cond, check_body,
        (jnp.int32(0), lo0, hi0, n_adm, settled0, jnp.max(1.0 - settled0)))
    thr = jnp.broadcast_to(jnp.where(n_adm <= kf, jnp.finfo(_F32).min, lo), (tq, LANES))

    def out_body(j, carry):
        for c in range(ncol):
            sc = score_ref[j, :, c * LANES:(c + 1) * LANES]
            bias_ref[0, j, :, c * LANES:(c + 1) * LANES] = jnp.where(sc >= thr, 0.0, NEG_BIAS)
        return carry

    lax.fori_loop(0, nkb, out_body, 0)

    def fill_body(j, carry):
        bias_ref[0, j] = jnp.full((tq, kb), NEG_BIAS, _F32)
        return carry

    lax.fori_loop(nkb, nkb_total, fill_body, 0)


def _index_bias(qi_hm, ki_rot, w, tq, kb, topk):
    nb = qi_hm.shape[0]
    s = ki_rot.shape[0]
    nkb_total = s // kb
    ki3 = ki_rot.reshape(nkb_total, kb, IDX_HEAD_DIM)
    kern = functools.partial(_index_kernel, tq=tq, kb=kb, nkb_total=nkb_total, topk=topk,
                             head_group=8)
    return pl.pallas_call(
        kern,
        grid=(nb,),
        in_specs=[pl.BlockSpec((1, IDX_HEADS, tq, IDX_HEAD_DIM), lambda i: (i, 0, 0, 0)),
                  pl.BlockSpec((nkb_total, kb, IDX_HEAD_DIM), lambda i: (0, 0, 0)),
                  pl.BlockSpec((tq, IDX_HEADS), lambda i: (i, 0))],
        out_specs=pl.BlockSpec((1, nkb_total, tq, kb), lambda i: (i, 0, 0, 0)),
        out_shape=jax.ShapeDtypeStruct((nb, nkb_total, tq, kb), _F32),
        scratch_shapes=[pltpu.VMEM((IDX_HEADS * tq, LANES), _F32),
                        pltpu.VMEM((nkb_total, tq, kb), _F32)],
        compiler_params=_params("parallel"),
        name="index_topk_bias",
    )(qi_hm, ki3, w)


def _attn_kernel(q_ref, k_ref, v_ref, b_ref, g_ref, o_ref, m_ref, l_ref, acc_ref, *,
                 heads, tq, kb, scale):
    i = pl.program_id(0)
    j = pl.program_id(1)
    last = ((i + 1) * tq - 1) // kb
    nrep = kb // LANES

    @pl.when(j == 0)
    def _():
        m_ref[...] = jnp.full(m_ref.shape, NEG_BIAS, _F32)
        l_ref[...] = jnp.zeros(l_ref.shape, _F32)
        acc_ref[...] = jnp.zeros(acc_ref.shape, _F32)

    c = scale * float(np.log2(np.e))

    @pl.when(j <= last)
    def _():
        bias = b_ref[...].reshape(tq, kb)
        for h in range(heads):
            sl = slice(h * ATTN_HEAD_DIM, (h + 1) * ATTN_HEAD_DIM)
            s = lax.dot_general(q_ref[:, sl], k_ref[:, sl], (((1,), (1,)), ((), ())),
                                preferred_element_type=_F32) + bias
            m_prev = m_ref[h]
            m_next = jnp.maximum(m_prev, jnp.max(s, axis=1, keepdims=True))
            alpha = jnp.exp2((m_prev - m_next) * c)
            p = jnp.exp2((s - jnp.concatenate([m_next] * nrep, axis=1)) * c)
            m_ref[h] = m_next
            pv = jnp.dot(p.astype(_BF16), v_ref[:, 2 * h * LANES:(2 * h + 2) * LANES],
                         preferred_element_type=_F32)
            acc_ref[:, sl] = alpha * acc_ref[:, sl] + pv[:, :LANES]
            l_ref[h] = alpha * l_ref[h] + pv[:, LANES:]

    @pl.when(j == last)
    def _():
        for h in range(heads):
            sl = slice(h * ATTN_HEAD_DIM, (h + 1) * ATTN_HEAD_DIM)
            g = g_ref[:, sl]
            o_ref[:, sl] = (acc_ref[:, sl] / l_ref[h] * (g * jax.nn.sigmoid(g))).astype(o_ref.dtype)


def _attention(q, k, v, bias, proj, aw, tq, kb, tq_idx):
    s = q.shape[0]
    heads = aw // ATTN_HEAD_DIM
    nq = s // tq
    nkb = s // kb
    rq = tq // tq_idx
    kmap = lambda i, j: (jnp.minimum(j, ((i + 1) * tq - 1) // kb), 0)
    kern = functools.partial(_attn_kernel, heads=heads, tq=tq, kb=kb,
                             scale=float(ATTN_HEAD_DIM ** -0.5))
    return pl.pallas_call(
        kern,
        grid=(nq, nkb),
        in_specs=[pl.BlockSpec((tq, aw), lambda i, j: (i, 0)),
                  pl.BlockSpec((kb, aw), kmap),
                  pl.BlockSpec((kb, 2 * aw), kmap),
                  pl.BlockSpec((rq, 1, tq_idx, kb),
                               lambda i, j: (i, jnp.minimum(j, ((i + 1) * tq - 1) // kb), 0, 0)),
                  pl.BlockSpec((tq, aw), lambda i, j: (i, 3))],
        out_specs=pl.BlockSpec((tq, aw), lambda i, j: (i, 0)),
        out_shape=jax.ShapeDtypeStruct((s, aw), _BF16),
        scratch_shapes=[pltpu.VMEM((heads, tq, LANES), _F32),
                        pltpu.VMEM((heads, tq, LANES), _F32),
                        pltpu.VMEM((tq, aw), _F32)],
        compiler_params=_params("parallel", "arbitrary"),
        name="masked_attention",
    )(q, k, v, bias, proj)


def _softplus(x):
    return jnp.maximum(x, 0.0) + jnp.log1p(jnp.exp(-jnp.abs(x)))


def _silu(x):
    return x * jax.nn.sigmoid(x)


def _dot_f32(a, b):
    return jnp.dot(a, b, preferred_element_type=_F32, precision=lax.Precision.HIGHEST)


def _ssd_kernel(z_ref, xs_ref, bm_ref, cm_ref, tail_ref, cw_ref, cb_ref, dtb_ref, alog_ref,
                dsk_ref, gain_ref, eh_ref, er_ref, tri_ref, o_ref,
                xs_ext, bm_ext, cm_ext, st_ref, *, blk, sw, gn, heads_per_group):
    i = pl.program_id(0)
    gw = heads_per_group * SSD_HEAD_DIM
    halo = 8

    @pl.when(i == 0)
    def _():
        xs_ext[0:halo, :] = jnp.zeros((halo, sw), _F32)
        bm_ext[0:halo, :] = jnp.zeros((halo, gn), _F32)
        cm_ext[0:halo, :] = jnp.zeros((halo, gn), _F32)
        st_ref[...] = jnp.zeros(st_ref.shape, _F32)

    def conv(ext, src_ref, off, width):
        ext[halo:halo + blk, :] = src_ref[...]
        acc = cb_ref[:, off:off + width]
        for t in range(CONV_WIDTH):
            start = halo - (CONV_WIDTH - 1) + t
            acc = acc + ext[start:start + blk, :] * cw_ref[t:t + 1, off:off + width]
        ext[0:halo, :] = ext[blk:blk + halo, :]
        return _silu(acc)

    xs_c = conv(xs_ext, xs_ref, 0, sw)
    bm_c = conv(bm_ext, bm_ref, sw, gn)
    cm_c = conv(cm_ext, cm_ref, sw + gn, gn)

    dt = _softplus(tail_ref[...] + dtb_ref[...])
    dta = dt * (-jnp.exp(alog_ref[...]))
    a_cum = _dot_f32(tri_ref[...], dta)
    a_cum_t = a_cum.T
    dt_e = _dot_f32(dt, eh_ref[...])
    acum_e = _dot_f32(a_cum, eh_ref[...])
    acum_rep = _dot_f32(a_cum, er_ref[...])
    a_last_e = acum_e[blk - 1:blk, :]

    xdt = xs_c * dt_e
    xw = (xdt * jnp.exp(a_last_e - acum_e)).astype(_BF16)
    xdt16 = xdt.astype(_BF16)
    exp_acum = jnp.exp(acum_e)
    bm16 = bm_c.astype(_BF16)
    cm16 = cm_c.astype(_BF16)

    rowi = lax.broadcasted_iota(jnp.int32, (blk, blk), 0)
    coli = lax.broadcasted_iota(jnp.int32, (blk, blk), 1)
    causal = rowi >= coli
    lane_g = lax.broadcasted_iota(jnp.int32, (blk, gw), 1)

    for g in range(SSD_GROUPS):
        bg = bm16[:, g * SSD_STATE:(g + 1) * SSD_STATE]
        cg = cm16[:, g * SSD_STATE:(g + 1) * SSD_STATE]
        cbm = lax.dot_general(cg, bg, (((1,), (1,)), ((), ())), preferred_element_type=_F32)
        x_g = xdt16[:, g * gw:(g + 1) * gw]
        lhs = []
        rhs = []
        for r in range(heads_per_group):
            h = g * heads_per_group + r
            seg = acum_rep[:, h * blk:(h + 1) * blk] - a_cum_t[LANES - 32 + h:LANES - 31 + h, :]
            decay = jnp.where(causal, jnp.exp(seg), 0.0)
            lhs.append((cbm * decay).astype(_BF16))
            in_head = (lane_g >= r * SSD_HEAD_DIM) & (lane_g < (r + 1) * SSD_HEAD_DIM)
            rhs.append(jnp.where(in_head, x_g, jnp.zeros_like(x_g)))
        y_diag = jnp.dot(jnp.concatenate(lhs, axis=1), jnp.concatenate(rhs, axis=0),
                         preferred_element_type=_F32)
        st = st_ref[g]
        y_off = jnp.dot(cg, st.astype(_BF16), preferred_element_type=_F32)
        sl = slice(g * gw, (g + 1) * gw)
        y = y_diag + y_off * exp_acum[:, sl] + xs_c[:, sl] * dsk_ref[:, sl]
        yg = y * _silu(z_ref[:, sl])
        ms = jnp.mean(yg * yg, axis=-1, keepdims=True)
        o_ref[:, sl] = (yg * lax.rsqrt(ms + NORM_EPS) * gain_ref[:, sl]).astype(o_ref.dtype)
        bg_t = bm_c[:, g * SSD_STATE:(g + 1) * SSD_STATE].T.astype(_BF16)
        upd = jnp.dot(bg_t, xw[:, sl], preferred_element_type=_F32)
        st_ref[g] = st * jnp.exp(a_last_e[:, sl]) + upd


def _ssd(proj, tail, conv_w, conv_b, dt_bias, a_log, d_skip, norm_gain, sw, col_z, blk):
    s = proj.shape[0]
    heads = sw // SSD_HEAD_DIM
    hpg = heads // SSD_GROUPS
    gn = SSD_GROUPS * SSD_STATE
    gw = hpg * SSD_HEAD_DIM
    assert blk == LANES and heads <= 32 and sw // SSD_GROUPS == gw
    conv_ch = sw + 2 * gn
    pad = LANES - 32
    place = lambda v: jnp.zeros((1, LANES), _F32).at[0, pad:pad + heads].set(v)
    dtb = place(dt_bias)
    alog = place(a_log)
    dsk = jnp.repeat(d_skip, SSD_HEAD_DIM).reshape(1, sw)
    eh = np.zeros((LANES, sw), np.float32)
    er = np.zeros((LANES, heads * blk), np.float32)
    for h in range(heads):
        eh[pad + h, h * SSD_HEAD_DIM:(h + 1) * SSD_HEAD_DIM] = 1.0
        er[pad + h, h * blk:(h + 1) * blk] = 1.0
    tri = np.tril(np.ones((blk, blk), np.float32))
    assert col_z % sw == 0 and (col_z + 2 * sw) % gn == 0
    z_blk = col_z // sw
    b_blk = (col_z + 2 * sw) // gn
    full = lambda shape: pl.BlockSpec(shape, lambda i: (0, 0))
    kern = functools.partial(_ssd_kernel, blk=blk, sw=sw, gn=gn, heads_per_group=hpg)
    return pl.pallas_call(
        kern,
        grid=(s // blk,),
        in_specs=[pl.BlockSpec((blk, sw), lambda i: (i, z_blk)),
                  pl.BlockSpec((blk, sw), lambda i: (i, z_blk + 1)),
                  pl.BlockSpec((blk, gn), lambda i: (i, b_blk)),
                  pl.BlockSpec((blk, gn), lambda i: (i, b_blk + 1)),
                  pl.BlockSpec((blk, LANES), lambda i: (i, 0)),
                  full((CONV_WIDTH, conv_ch)), full((1, conv_ch)),
                  full((1, LANES)), full((1, LANES)), full((1, sw)), full((1, sw)),
                  full((LANES, sw)), full((LANES, heads * blk)), full((blk, blk))],
        out_specs=pl.BlockSpec((blk, sw), lambda i: (i, 0)),
        out_shape=jax.ShapeDtypeStruct((s, sw), _BF16),
        scratch_shapes=[pltpu.VMEM((blk + 8, sw), _F32),
                        pltpu.VMEM((blk + 8, gn), _F32),
                        pltpu.VMEM((blk + 8, gn), _F32),
                        pltpu.VMEM((SSD_GROUPS, SSD_STATE, gw), _F32)],
        compiler_params=_params("arbitrary"),
        name="ssd_mixer",
    )(proj, proj, proj, proj, tail, conv_w, conv_b.reshape(1, conv_ch), dtb, alog, dsk,
      norm_gain.reshape(1, sw), jnp.asarray(eh), jnp.asarray(er), jnp.asarray(tri))


def _merge_kernel(ya_ref, yb_ref, wa_ref, wb_ref, la_ref, lb_ref, o_ref):
    pa = jnp.dot(ya_ref[...], wa_ref[...], preferred_element_type=_F32)
    pb = jnp.dot(yb_ref[...], wb_ref[...], preferred_element_type=_F32)
    o_ref[...] = (jax.nn.sigmoid(la_ref[...]) * pa
                  + jax.nn.sigmoid(lb_ref[...]) * pb).astype(o_ref.dtype)


def _merge(ya, yb, wa, wb, proj, col_merge, tm, tn):
    s, aw = ya.shape
    d = wa.shape[1]
    tm = min(tm, s)
    tn = min(tn, d)
    assert col_merge % tn == 0 and d % tn == 0
    ca = col_merge // tn
    cb = (col_merge + d) // tn
    return pl.pallas_call(
        _merge_kernel,
        grid=(s // tm, d // tn),
        in_specs=[pl.BlockSpec((tm, aw), lambda i, j: (i, 0)),
                  pl.BlockSpec((tm, yb.shape[1]), lambda i, j: (i, 0)),
                  pl.BlockSpec((aw, tn), lambda i, j: (0, j)),
                  pl.BlockSpec((yb.shape[1], tn), lambda i, j: (0, j)),
                  pl.BlockSpec((tm, tn), lambda i, j: (i, ca + j)),
                  pl.BlockSpec((tm, tn), lambda i, j: (i, cb + j))],
        out_specs=pl.BlockSpec((tm, tn), lambda i, j: (i, j)),
        out_shape=jax.ShapeDtypeStruct((s, d), _BF16),
        compiler_params=_params("parallel", "arbitrary"),
        name="gated_merge",
    )(ya, yb, wa, wb, proj, proj)


def _postnorm_kernel(x_ref, o_in_ref, g_ref, o_ref):
    o = o_in_ref[...]
    ms = jnp.mean(o * o, axis=-1, keepdims=True)
    o_ref[...] = x_ref[...] + o * lax.rsqrt(ms + NORM_EPS) * g_ref[...]


def _postnorm(x, out, gain, tm):
    s, d = x.shape
    blk = pl.BlockSpec((tm, d), lambda i: (i, 0))
    return pl.pallas_call(
        _postnorm_kernel,
        grid=(s // tm,),
        in_specs=[blk, blk, pl.BlockSpec((1, d), lambda i: (0, 0))],
        out_specs=blk,
        out_shape=jax.ShapeDtypeStruct((s, d), _F32),
        compiler_params=_params("parallel"),
        name="post_rmsnorm_residual",
    )(x, out, gain.reshape(1, d))


def _rope_tables(seq_len):
    def tab(dim):
        inv_freq = 1.0 / (ROPE_THETA ** (jnp.arange(0, dim, 2, dtype=_F32) / dim))
        ang = jnp.arange(seq_len, dtype=_F32)[:, None] * inv_freq[None, :]
        return jnp.cos(ang), jnp.sin(ang)
    ca, sa = tab(ATTN_HEAD_DIM)
    ci, si = tab(IDX_HEAD_DIM)
    return (jnp.concatenate([ca, ca], axis=1), jnp.concatenate([-sa, sa], axis=1),
            jnp.concatenate([ci, ci, ci, ci], axis=1), jnp.concatenate([-si, si, -si, si], axis=1))


def _layer(x2, pre_gain, w_in, conv_w, conv_b, dt_bias, a_log, d_skip, ssd_gain,
           w_a, w_b, w_out, post_gain, tabs):
    s, d = x2.shape
    aw = d // 2
    sw = d // 2
    gn = SSD_GROUPS * SSD_STATE
    ssd_heads = sw // SSD_HEAD_DIM
    qi_w = IDX_HEADS * IDX_HEAD_DIM
    sizes = (aw, aw, aw, aw, qi_w, IDX_HEAD_DIM, IDX_HEADS, sw, sw, gn, gn, ssd_heads, 2 * d)
    off = np.concatenate([[0], np.cumsum(sizes)])
    cols = lambda a, b: w_in[:, int(off[a]):int(off[b])].astype(_BF16)
    assert IDX_HEAD_DIM + IDX_HEADS + ssd_heads <= LANES
    w_attn = cols(0, 5)
    w_ssd = cols(7, 11)
    w_gate = cols(12, 13)
    w_tail = jnp.concatenate(
        [cols(5, 7), cols(11, 12), jnp.zeros((d, 32 - ssd_heads), _BF16)], axis=1)

    h = _rmsnorm(x2, pre_gain, tm=256)
    proj_a = _matmul(h, w_attn, 1024, 1024, _F32, "in_proj_attn")
    proj_s = _matmul(h, w_ssd, 1024, 1024, _F32, "in_proj_ssd")
    proj_g = _matmul(h, w_gate, 1024, 1024, _F32, "in_proj_gate")
    tail = _matmul(h, w_tail, 1024, LANES, _F32, "in_proj_tail")

    tq_idx = 128
    q, k, v, qi_hm, ki_rot, w_idx = _prep(proj_a, tail, tabs, aw, tq_idx)
    topk = min(TOPK_MAX, s // 4)
    kb = min(512, s)
    bias = _index_bias(qi_hm, ki_rot, w_idx, tq_idx, kb, topk)
    y_a = _attention(q, k, v, bias, proj_a, aw, min(512, s), kb, tq_idx)

    y_b = _ssd(proj_s, tail, conv_w, conv_b, dt_bias, a_log, d_skip, ssd_gain, sw, 0, LANES)

    merged = _merge(y_a, y_b, w_a.astype(_BF16), w_b.astype(_BF16), proj_g, 0, 512, 1024)
    out = _matmul(merged, w_out.astype(_BF16), 1024, 1024, _F32, "out_proj")
    return _postnorm(x2, out, post_gain, tm=256)


def kernel(x, pre_norm_gain, w_in, conv_w, conv_b, dt_bias, a_log, d_skip, ssd_norm_gain,
           w_branch_attn, w_branch_ssd, w_out, post_norm_gain):
    b, s, d = x.shape
    tabs = _rope_tables(s)
    outs = []
    for bi in range(b):
        xb = x[bi]
        for layer in range(pre_norm_gain.shape[0]):
            xb = _layer(xb, pre_norm_gain[layer], w_in[layer], conv_w[layer], conv_b[layer],
                        dt_bias[layer], a_log[layer], d_skip[layer], ssd_norm_gain[layer],
                        w_branch_attn[layer], w_branch_ssd[layer], w_out[layer],
                        post_norm_gain[layer], tabs)
        outs.append(xb)
    return jnp.stack(outs, axis=0)
```

```python
import functools

import numpy as np
import jax
import jax.numpy as jnp
from jax import lax
from jax.experimental import pallas as pl
from jax.experimental.pallas import tpu as pltpu

CHUNK = 64
ROPE_THETA = 10000.0
NORM_EPS = 1e-6
ATTN_HEAD_DIM = 128
IDX_HEADS = 32
IDX_HEAD_DIM = 64
TOPK_MAX = 256
SSD_HEAD_DIM = 64
SSD_GROUPS = 8
SSD_STATE = 128
CONV_WIDTH = 4

LANES = 128
VMEM_LIMIT_BYTES = 56 * 1024 * 1024

NEG_BIAS = -1e30

_F32 = jnp.float32
_BF16 = jnp.bfloat16


def _params(*sem):
    return pltpu.CompilerParams(dimension_semantics=sem, vmem_limit_bytes=VMEM_LIMIT_BYTES)


def _rmsnorm_kernel(x_ref, g_ref, o_ref):
    x = x_ref[...]
    ms = jnp.mean(x * x, axis=-1, keepdims=True)
    o_ref[...] = (x * lax.rsqrt(ms + NORM_EPS) * g_ref[...]).astype(o_ref.dtype)


def _rmsnorm(x, gain, tm):
    s, d = x.shape
    return pl.pallas_call(
        _rmsnorm_kernel,
        grid=(s // tm,),
        in_specs=[pl.BlockSpec((tm, d), lambda i: (i, 0)),
                  pl.BlockSpec((1, d), lambda i: (0, 0))],
        out_specs=pl.BlockSpec((tm, d), lambda i: (i, 0)),
        out_shape=jax.ShapeDtypeStruct((s, d), _BF16),
        compiler_params=_params("parallel"),
        name="pre_rmsnorm",
    )(x, gain.reshape(1, d))


def _matmul_kernel(a_ref, b_ref, o_ref):
    o_ref[...] = jnp.dot(a_ref[...], b_ref[...],
                         preferred_element_type=_F32).astype(o_ref.dtype)


def _cast_kernel(w_ref, o_ref):
    o_ref[...] = w_ref[...].astype(o_ref.dtype)


def _cast_bf16(w, tr):
    rows, n = w.shape
    blk = pl.BlockSpec((tr, n), lambda i: (i, 0))
    return pl.pallas_call(
        _cast_kernel,
        grid=(rows // tr,),
        in_specs=[blk],
        out_specs=blk,
        out_shape=jax.ShapeDtypeStruct((rows, n), _BF16),
        compiler_params=_params("parallel"),
        name="weight_cast",
    )(w)


def _matmul(a, b, tm, tn, out_dtype, name, n=None):
    m, k = a.shape
    n = b.shape[1] if n is None else n
    tm = min(tm, m)
    tn = min(tn, n)
    assert m % tm == 0 and n % tn == 0
    return pl.pallas_call(
        _matmul_kernel,
        grid=(m // tm, n // tn),
        in_specs=[pl.BlockSpec((tm, k), lambda i, j: (i, 0)),
                  pl.BlockSpec((k, tn), lambda i, j: (0, j))],
        out_specs=pl.BlockSpec((tm, tn), lambda i, j: (i, j)),
        out_shape=jax.ShapeDtypeStruct((m, n), out_dtype),
        compiler_params=_params("parallel", "arbitrary"),
        name=name,
    )(a, b)


def _rope_pairs64(x, c4, s4):
    lane = lax.broadcasted_iota(jnp.int32, x.shape, 1)
    first_half = (lane % IDX_HEAD_DIM) < (IDX_HEAD_DIM // 2)
    partner = jnp.where(first_half,
                        pltpu.roll(x, LANES - IDX_HEAD_DIM // 2, 1),
                        pltpu.roll(x, IDX_HEAD_DIM // 2, 1))
    return x * c4 + partner * s4


def _prep_kernel(q_ref, k_ref, v_ref, qi_ref, tail_ref, ca_ref, sa_ref, ci_ref, si_ref,
                 qo_ref, ko_ref, vo_ref, qio_ref, kio_ref, wo_ref, *, heads, w_scale):
    ca = ca_ref[...]
    sa = sa_ref[...]
    for h in range(heads):
        sl = slice(h * LANES, (h + 1) * LANES)
        xq = q_ref[:, sl]
        qo_ref[:, sl] = (xq * ca + pltpu.roll(xq, LANES // 2, 1) * sa).astype(qo_ref.dtype)
        xk = k_ref[:, sl]
        ko_ref[:, sl] = (xk * ca + pltpu.roll(xk, LANES // 2, 1) * sa).astype(ko_ref.dtype)
    ones = jnp.ones((v_ref.shape[0], LANES), vo_ref.dtype)
    for h in range(heads):
        vo_ref[:, 2 * h * LANES:(2 * h + 1) * LANES] = v_ref[:, h * LANES:(h + 1) * LANES].astype(
            vo_ref.dtype)
        vo_ref[:, (2 * h + 1) * LANES:(2 * h + 2) * LANES] = ones
    ci = ci_ref[...]
    si = si_ref[...]
    for g in range(IDX_HEADS // 2):
        r = _rope_pairs64(qi_ref[:, g * LANES:(g + 1) * LANES], ci, si).astype(qio_ref.dtype)
        qio_ref[0, 2 * g] = r[:, :IDX_HEAD_DIM]
        qio_ref[0, 2 * g + 1] = r[:, IDX_HEAD_DIM:]
    t = tail_ref[...]
    kio_ref[...] = _rope_pairs64(t, ci, si)[:, :IDX_HEAD_DIM].astype(kio_ref.dtype)
    wo_ref[...] = t[:, IDX_HEAD_DIM:IDX_HEAD_DIM + IDX_HEADS] * w_scale


def _prep(proj, tail, tabs, aw, tq):
    s = proj.shape[0]
    heads = aw // ATTN_HEAD_DIM
    qi_w = IDX_HEADS * IDX_HEAD_DIM
    qi_blk = (4 * aw) // qi_w
    assert (4 * aw) % qi_w == 0
    nb = s // tq
    row = lambda i: (i, 0)
    kern = functools.partial(_prep_kernel, heads=heads,
                             w_scale=float(IDX_HEADS ** -0.5 * IDX_HEAD_DIM ** -0.5))
    return pl.pallas_call(
        kern,
        grid=(nb,),
        in_specs=[pl.BlockSpec((tq, aw), lambda i: (i, 0)),
                  pl.BlockSpec((tq, aw), lambda i: (i, 1)),
                  pl.BlockSpec((tq, aw), lambda i: (i, 2)),
                  pl.BlockSpec((tq, qi_w), lambda i: (i, qi_blk)),
                  pl.BlockSpec((tq, LANES), row),
                  pl.BlockSpec((tq, LANES), row),
                  pl.BlockSpec((tq, LANES), row),
                  pl.BlockSpec((tq, LANES), row),
                  pl.BlockSpec((tq, LANES), row)],
        out_specs=[pl.BlockSpec((tq, aw), row),
                   pl.BlockSpec((tq, aw), row),
                   pl.BlockSpec((tq, 2 * aw), row),
                   pl.BlockSpec((1, IDX_HEADS, tq, IDX_HEAD_DIM), lambda i: (i, 0, 0, 0)),
                   pl.BlockSpec((tq, IDX_HEAD_DIM), row),
                   pl.BlockSpec((tq, IDX_HEADS), row)],
        out_shape=[jax.ShapeDtypeStruct((s, aw), _BF16),
                   jax.ShapeDtypeStruct((s, aw), _BF16),
                   jax.ShapeDtypeStruct((s, 2 * aw), _BF16),
                   jax.ShapeDtypeStruct((nb, IDX_HEADS, tq, IDX_HEAD_DIM), _BF16),
                   jax.ShapeDtypeStruct((s, IDX_HEAD_DIM), _BF16),
                   jax.ShapeDtypeStruct((s, IDX_HEADS), _F32)],
        compiler_params=_params("parallel"),
        name="rope_prep",
    )(proj, proj, proj, proj, tail, *tabs)


def _index_kernel(qi_ref, ki_ref, w_ref, bias_ref, wb_ref, score_ref, *, tq, kb, nkb_total, topk,
                  head_group):
    i = pl.program_id(0)
    nkb = ((i + 1) * tq + kb - 1) // kb
    ncol = kb // LANES

    w = w_ref[...]
    for h in range(IDX_HEADS):
        wb_ref[h * tq:(h + 1) * tq, :] = jnp.broadcast_to(w[:, h:h + 1], (tq, LANES))

    row = lax.broadcasted_iota(jnp.int32, (tq, LANES), 0)
    lane = lax.broadcasted_iota(jnp.int32, (tq, LANES), 1)
    vis_end = i * tq + (row // CHUNK + 1) * CHUNK

    def score_body(j, carry):
        rmax, rmin = carry
        kj = ki_ref[j]
        acc = [jnp.zeros((tq, LANES), _F32) for _ in range(ncol)]
        for hg in range(IDX_HEADS // head_group):
            q = qi_ref[0, hg * head_group:(hg + 1) * head_group].reshape(
                head_group * tq, IDX_HEAD_DIM)
            logit = lax.dot_general(q, kj, (((1,), (1,)), ((), ())),
                                    preferred_element_type=_F32)
            wb = wb_ref[hg * head_group * tq:(hg + 1) * head_group * tq, :]
            for c in range(ncol):
                r = jnp.maximum(logit[:, c * LANES:(c + 1) * LANES], 0.0) * wb
                acc[c] = acc[c] + jnp.sum(r.reshape(head_group, tq, LANES), axis=0)
        for c in range(ncol):
            admissible = (j * kb + c * LANES + lane) < vis_end
            sc = jnp.where(admissible, acc[c], -jnp.inf)
            score_ref[j, :, c * LANES:(c + 1) * LANES] = sc
            rmax = jnp.maximum(rmax, sc)
            rmin = jnp.minimum(rmin, jnp.where(admissible, acc[c], jnp.inf))
        return rmax, rmin

    rmax, rmin = lax.fori_loop(
        0, nkb, score_body,
        (jnp.full((tq, LANES), -jnp.inf, _F32), jnp.full((tq, LANES), jnp.inf, _F32)))

    kf = float(topk)
    n_adm = vis_end[:, :1].astype(_F32)
    lo0 = jnp.min(rmin, axis=1, keepdims=True)
    hi0 = jnp.max(rmax, axis=1, keepdims=True)
    steps_per_check = 2
    max_checks = 160

    def bisect_step(lo, hi, clo, settled):
        mid = 0.5 * lo + 0.5 * hi
        mid_b = jnp.broadcast_to(mid, (tq, LANES))

        def cnt_body(j, cnt):
            for c in range(ncol):
                sc = score_ref[j, :, c * LANES:(c + 1) * LANES]
                cnt = cnt + jnp.where(sc >= mid_b, 1.0, 0.0)
            return cnt

        cnt = lax.fori_loop(0, nkb, cnt_body, jnp.zeros((tq, LANES), _F32))
        tot = jnp.sum(cnt, axis=1, keepdims=True)
        accept = tot >= kf
        stalled = jnp.logical_or(mid <= lo, mid >= hi)
        lo = jnp.where(accept, mid, lo)
        clo = jnp.where(accept, tot, clo)
        hi = jnp.where(accept, hi, mid)
        settled = jnp.maximum(settled, jnp.where(jnp.logical_or(clo == kf, stalled), 1.0, 0.0))
        return lo, hi, clo, settled

    def check_cond(state):
        n, _, _, _, _, pending = state
        return jnp.logical_and(n < max_checks, pending > 0.0)

    def check_body(state):
        n, lo, hi, clo, settled, _ = state
        for _ in range(steps_per_check):
            lo, hi, clo, settled = bisect_step(lo, hi, clo, settled)
        return n + 1, lo, hi, clo, settled, jnp.max(1.0 - settled)

    settled0 = jnp.where(n_adm <= kf, 1.0, 0.0)
    _, lo, _, _, _, _ = lax.while_loop(
        check_cond, check_body,
        (jnp.int32(0), lo0, hi0, n_adm, settled0, jnp.max(1.0 - settled0)))
    thr = jnp.broadcast_to(jnp.where(n_adm <= kf, jnp.finfo(_F32).min, lo), (tq, LANES))

    def out_body(j, carry):
        for c in range(ncol):
            sc = score_ref[j, :, c * LANES:(c + 1) * LANES]
            bias_ref[0, j, :, c * LANES:(c + 1) * LANES] = jnp.where(sc >= thr, 0.0, NEG_BIAS)
        return carry

    lax.fori_loop(0, nkb, out_body, 0)

    def fill_body(j, carry):
        bias_ref[0, j] = jnp.full((tq, kb), NEG_BIAS, _F32)
        return carry

    lax.fori_loop(nkb, nkb_total, fill_body, 0)


def _index_bias(qi_hm, ki_rot, w, tq, kb, topk):
    nb = qi_hm.shape[0]
    s = ki_rot.shape[0]
    nkb_total = s // kb
    ki3 = ki_rot.reshape(nkb_total, kb, IDX_HEAD_DIM)
    kern = functools.partial(_index_kernel, tq=tq, kb=kb, nkb_total=nkb_total, topk=topk,
                             head_group=8)
    return pl.pallas_call(
        kern,
        grid=(nb,),
        in_specs=[pl.BlockSpec((1, IDX_HEADS, tq, IDX_HEAD_DIM), lambda i: (i, 0, 0, 0)),
                  pl.BlockSpec((nkb_total, kb, IDX_HEAD_DIM), lambda i: (0, 0, 0)),
                  pl.BlockSpec((tq, IDX_HEADS), lambda i: (i, 0))],
        out_specs=pl.BlockSpec((1, nkb_total, tq, kb), lambda i: (i, 0, 0, 0)),
        out_shape=jax.ShapeDtypeStruct((nb, nkb_total, tq, kb), _F32),
        scratch_shapes=[pltpu.VMEM((IDX_HEADS * tq, LANES), _F32),
                        pltpu.VMEM((nkb_total, tq, kb), _F32)],
        compiler_params=_params("parallel"),
        name="index_topk_bias",
    )(qi_hm, ki3, w)


def _attn_kernel(q_ref, k_ref, v_ref, b_ref, g_ref, o_ref, m_ref, l_ref, acc_ref, *,
                 heads, tq, kb, scale):
    i = pl.program_id(0)
    j = pl.program_id(1)
    last = ((i + 1) * tq - 1) // kb
    nrep = kb // LANES

    @pl.when(j == 0)
    def _():
        m_ref[...] = jnp.full(m_ref.shape, NEG_BIAS, _F32)
        l_ref[...] = jnp.zeros(l_ref.shape, _F32)
        acc_ref[...] = jnp.zeros(acc_ref.shape, _F32)

    c = scale * float(np.log2(np.e))

    @pl.when(j <= last)
    def _():
        bias = b_ref[...].reshape(tq, kb)
        for h in range(heads):
            sl = slice(h * ATTN_HEAD_DIM, (h + 1) * ATTN_HEAD_DIM)
            s = lax.dot_general(q_ref[:, sl], k_ref[:, sl], (((1,), (1,)), ((), ())),
                                preferred_element_type=_F32) + bias
            m_prev = m_ref[h]
            m_next = jnp.maximum(m_prev, jnp.max(s, axis=1, keepdims=True))
            alpha = jnp.exp2((m_prev - m_next) * c)
            p = jnp.exp2((s - jnp.concatenate([m_next] * nrep, axis=1)) * c)
            m_ref[h] = m_next
            pv = jnp.dot(p.astype(_BF16), v_ref[:, 2 * h * LANES:(2 * h + 2) * LANES],
                         preferred_element_type=_F32)
            acc_ref[:, sl] = alpha * acc_ref[:, sl] + pv[:, :LANES]
            l_ref[h] = alpha * l_ref[h] + pv[:, LANES:]

    @pl.when(j == last)
    def _():
        for h in range(heads):
            sl = slice(h * ATTN_HEAD_DIM, (h + 1) * ATTN_HEAD_DIM)
            g = g_ref[:, sl]
            o_ref[:, sl] = (acc_ref[:, sl] / l_ref[h] * (g * jax.nn.sigmoid(g))).astype(o_ref.dtype)


def _attention(q, k, v, bias, proj, aw, tq, kb, tq_idx):
    s = q.shape[0]
    heads = aw // ATTN_HEAD_DIM
    nq = s // tq
    nkb = s // kb
    rq = tq // tq_idx
    kmap = lambda i, j: (jnp.minimum(j, ((i + 1) * tq - 1) // kb), 0)
    kern = functools.partial(_attn_kernel, heads=heads, tq=tq, kb=kb,
                             scale=float(ATTN_HEAD_DIM ** -0.5))
    return pl.pallas_call(
        kern,
        grid=(nq, nkb),
        in_specs=[pl.BlockSpec((tq, aw), lambda i, j: (i, 0)),
                  pl.BlockSpec((kb, aw), kmap),
                  pl.BlockSpec((kb, 2 * aw), kmap),
                  pl.BlockSpec((rq, 1, tq_idx, kb),
                               lambda i, j: (i, jnp.minimum(j, ((i + 1) * tq - 1) // kb), 0, 0)),
                  pl.BlockSpec((tq, aw), lambda i, j: (i, 3))],
        out_specs=pl.BlockSpec((tq, aw), lambda i, j: (i, 0)),
        out_shape=jax.ShapeDtypeStruct((s, aw), _BF16),
        scratch_shapes=[pltpu.VMEM((heads, tq, LANES), _F32),
                        pltpu.VMEM((heads, tq, LANES), _F32),
                        pltpu.VMEM((tq, aw), _F32)],
        compiler_params=_params("parallel", "arbitrary"),
        name="masked_attention",
    )(q, k, v, bias, proj)


def _softplus(x):
    return jnp.maximum(x, 0.0) + jnp.log1p(jnp.exp(-jnp.abs(x)))


def _silu(x):
    return x * jax.nn.sigmoid(x)


def _dot_f32(a, b):
    return jnp.dot(a, b, preferred_element_type=_F32, precision=lax.Precision.HIGHEST)


def _select_columns(x, onehot16):
    x1 = x.astype(_BF16)
    r1 = x - x1.astype(_F32)
    x2 = r1.astype(_BF16)
    x3 = (r1 - x2.astype(_F32)).astype(_BF16)
    dot = lambda a: jnp.dot(a, onehot16, preferred_element_type=_F32)
    return dot(x1) + dot(x2) + dot(x3)


def _ssd_kernel(z_ref, xs_ref, bm_ref, cm_ref, tail_ref, cw_ref, cb_ref, dtb_ref, alog_ref,
                dsk_ref, gain_ref, eh_ref, tri_ref, o_ref,
                xs_ext, bm_ext, cm_ext, st_ref, *, blk, sw, gn, heads_per_group):
    i = pl.program_id(0)
    gw = heads_per_group * SSD_HEAD_DIM
    halo = 8

    @pl.when(i == 0)
    def _():
        xs_ext[0:halo, :] = jnp.zeros((halo, sw), _F32)
        bm_ext[0:halo, :] = jnp.zeros((halo, gn), _F32)
        cm_ext[0:halo, :] = jnp.zeros((halo, gn), _F32)
        st_ref[...] = jnp.zeros(st_ref.shape, _F32)

    def conv(ext, src_ref, off, width):
        ext[halo:halo + blk, :] = src_ref[...]
        acc = cb_ref[:, off:off + width]
        for t in range(CONV_WIDTH):
            start = halo - (CONV_WIDTH - 1) + t
            acc = acc + ext[start:start + blk, :] * cw_ref[t:t + 1, off:off + width]
        ext[0:halo, :] = ext[blk:blk + halo, :]
        return _silu(acc)

    xs_c = conv(xs_ext, xs_ref, 0, sw)
    bm_c = conv(bm_ext, bm_ref, sw, gn)
    cm_c = conv(cm_ext, cm_ref, sw + gn, gn)

    dt = _softplus(tail_ref[...] + dtb_ref[...])
    dta = dt * (-jnp.exp(alog_ref[...]))
    a_cum = _dot_f32(tri_ref[...], dta)
    a_cum_t = a_cum.T
    dt_e = _select_columns(dt, eh_ref[...])
    acum_e = _select_columns(a_cum, eh_ref[...])
    a_last_e = acum_e[blk - 1:blk, :]

    xdt = xs_c * dt_e
    xw = (xdt * jnp.exp(a_last_e - acum_e)).astype(_BF16)
    xdt16 = xdt.astype(_BF16)
    exp_acum = jnp.exp(acum_e)
    bm16 = bm_c.astype(_BF16)
    cm16 = cm_c.astype(_BF16)

    rowi = lax.broadcasted_iota(jnp.int32, (blk, blk), 0)
    coli = lax.broadcasted_iota(jnp.int32, (blk, blk), 1)
    causal = rowi >= coli
    lane_g = lax.broadcasted_iota(jnp.int32, (blk, gw), 1)

    for g in range(SSD_GROUPS):
        bg = bm16[:, g * SSD_STATE:(g + 1) * SSD_STATE]
        cg = cm16[:, g * SSD_STATE:(g + 1) * SSD_STATE]
        cbm = lax.dot_general(cg, bg, (((1,), (1,)), ((), ())), preferred_element_type=_F32)
        x_g = xdt16[:, g * gw:(g + 1) * gw]
        lhs = []
        rhs = []
        for r in range(heads_per_group):
            h = g * heads_per_group + r
            col = LANES - 32 + h
            seg = (jnp.broadcast_to(a_cum[:, col:col + 1], (blk, blk))
                   - a_cum_t[col:col + 1, :])
            decay = jnp.where(causal, jnp.exp(seg), 0.0)
            lhs.append((cbm * decay).astype(_BF16))
            in_head = (lane_g >= r * SSD_HEAD_DIM) & (lane_g < (r + 1) * SSD_HEAD_DIM)
            rhs.append(jnp.where(in_head, x_g, jnp.zeros_like(x_g)))
        y_diag = jnp.dot(jnp.concatenate(lhs, axis=1), jnp.concatenate(rhs, axis=0),
                         preferred_element_type=_F32)
        st = st_ref[g]
        y_off = jnp.dot(cg, st.astype(_BF16), preferred_element_type=_F32)
        sl = slice(g * gw, (g + 1) * gw)
        y = y_diag + y_off * exp_acum[:, sl] + xs_c[:, sl] * dsk_ref[:, sl]
        yg = y * _silu(z_ref[:, sl])
        ms = jnp.mean(yg * yg, axis=-1, keepdims=True)
        o_ref[:, sl] = (yg * lax.rsqrt(ms + NORM_EPS) * gain_ref[:, sl]).astype(o_ref.dtype)
        bg_t = bm_c[:, g * SSD_STATE:(g + 1) * SSD_STATE].T.astype(_BF16)
        upd = jnp.dot(bg_t, xw[:, sl], preferred_element_type=_F32)
        st_ref[g] = st * jnp.exp(a_last_e[:, sl]) + upd


def _ssd(proj, tail, conv_w, conv_b, dt_bias, a_log, d_skip, norm_gain, sw, col_z, blk):
    s = proj.shape[0]
    heads = sw // SSD_HEAD_DIM
    hpg = heads // SSD_GROUPS
    gn = SSD_GROUPS * SSD_STATE
    gw = hpg * SSD_HEAD_DIM
    assert blk == LANES and heads <= 32 and sw // SSD_GROUPS == gw
    conv_ch = sw + 2 * gn
    pad = LANES - 32
    place = lambda v: jnp.zeros((1, LANES), _F32).at[0, pad:pad + heads].set(v)
    dtb = place(dt_bias)
    alog = place(a_log)
    dsk = jnp.repeat(d_skip, SSD_HEAD_DIM).reshape(1, sw)
    eh = np.zeros((LANES, sw), np.float32)
    for h in range(heads):
        eh[pad + h, h * SSD_HEAD_DIM:(h + 1) * SSD_HEAD_DIM] = 1.0
    tri = np.tril(np.ones((blk, blk), np.float32))
    assert col_z % sw == 0 and (col_z + 2 * sw) % gn == 0
    z_blk = col_z // sw
    b_blk = (col_z + 2 * sw) // gn
    full = lambda shape: pl.BlockSpec(shape, lambda i: (0, 0))
    kern = functools.partial(_ssd_kernel, blk=blk, sw=sw, gn=gn, heads_per_group=hpg)
    return pl.pallas_call(
        kern,
        grid=(s // blk,),
        in_specs=[pl.BlockSpec((blk, sw), lambda i: (i, z_blk)),
                  pl.BlockSpec((blk, sw), lambda i: (i, z_blk + 1)),
                  pl.BlockSpec((blk, gn), lambda i: (i, b_blk)),
                  pl.BlockSpec((blk, gn), lambda i: (i, b_blk + 1)),
                  pl.BlockSpec((blk, LANES), lambda i: (i, 0)),
                  full((CONV_WIDTH, conv_ch)), full((1, conv_ch)),
                  full((1, LANES)), full((1, LANES)), full((1, sw)), full((1, sw)),
                  full((LANES, sw)), full((blk, blk))],
        out_specs=pl.BlockSpec((blk, sw), lambda i: (i, 0)),
        out_shape=jax.ShapeDtypeStruct((s, sw), _BF16),
        scratch_shapes=[pltpu.VMEM((blk + 8, sw), _F32),
                        pltpu.VMEM((blk + 8, gn), _F32),
                        pltpu.VMEM((blk + 8, gn), _F32),
                        pltpu.VMEM((SSD_GROUPS, SSD_STATE, gw), _F32)],
        compiler_params=_params("arbitrary"),
        name="ssd_mixer",
    )(proj, proj, proj, proj, tail, conv_w, conv_b.reshape(1, conv_ch), dtb, alog, dsk,
      norm_gain.reshape(1, sw), jnp.asarray(eh, _BF16), jnp.asarray(tri))


def _merge_kernel(ya_ref, yb_ref, wa_ref, wb_ref, la_ref, lb_ref, o_ref):
    pa = jnp.dot(ya_ref[...], wa_ref[...], preferred_element_type=_F32)
    pb = jnp.dot(yb_ref[...], wb_ref[...], preferred_element_type=_F32)
    o_ref[...] = (jax.nn.sigmoid(la_ref[...]) * pa
                  + jax.nn.sigmoid(lb_ref[...]) * pb).astype(o_ref.dtype)


def _merge(ya, yb, wa, wb, proj, col_merge, tm, tn):
    s, aw = ya.shape
    d = wa.shape[1]
    tm = min(tm, s)
    tn = min(tn, d)
    assert col_merge % tn == 0 and d % tn == 0
    ca = col_merge // tn
    cb = (col_merge + d) // tn
    return pl.pallas_call(
        _merge_kernel,
        grid=(s // tm, d // tn),
        in_specs=[pl.BlockSpec((tm, aw), lambda i, j: (i, 0)),
                  pl.BlockSpec((tm, yb.shape[1]), lambda i, j: (i, 0)),
                  pl.BlockSpec((aw, tn), lambda i, j: (0, j)),
                  pl.BlockSpec((yb.shape[1], tn), lambda i, j: (0, j)),
                  pl.BlockSpec((tm, tn), lambda i, j: (i, ca + j)),
                  pl.BlockSpec((tm, tn), lambda i, j: (i, cb + j))],
        out_specs=pl.BlockSpec((tm, tn), lambda i, j: (i, j)),
        out_shape=jax.ShapeDtypeStruct((s, d), _BF16),
        compiler_params=_params("parallel", "arbitrary"),
        name="gated_merge",
    )(ya, yb, wa, wb, proj, proj)


def _postnorm_kernel(x_ref, o_in_ref, g_ref, o_ref):
    o = o_in_ref[...]
    ms = jnp.mean(o * o, axis=-1, keepdims=True)
    o_ref[...] = x_ref[...] + o * lax.rsqrt(ms + NORM_EPS) * g_ref[...]


def _postnorm(x, out, gain, tm):
    s, d = x.shape
    blk = pl.BlockSpec((tm, d), lambda i: (i, 0))
    return pl.pallas_call(
        _postnorm_kernel,
        grid=(s // tm,),
        in_specs=[blk, blk, pl.BlockSpec((1, d), lambda i: (0, 0))],
        out_specs=blk,
        out_shape=jax.ShapeDtypeStruct((s, d), _F32),
        compiler_params=_params("parallel"),
        name="post_rmsnorm_residual",
    )(x, out, gain.reshape(1, d))


def _rope_tables(seq_len):
    def tab(dim):
        inv_freq = 1.0 / (ROPE_THETA ** (jnp.arange(0, dim, 2, dtype=_F32) / dim))
        ang = jnp.arange(seq_len, dtype=_F32)[:, None] * inv_freq[None, :]
        return jnp.cos(ang), jnp.sin(ang)
    ca, sa = tab(ATTN_HEAD_DIM)
    ci, si = tab(IDX_HEAD_DIM)
    return (jnp.concatenate([ca, ca], axis=1), jnp.concatenate([-sa, sa], axis=1),
            jnp.concatenate([ci, ci, ci, ci], axis=1), jnp.concatenate([-si, si, -si, si], axis=1))


def _layer(x2, pre_gain, w_in, conv_w, conv_b, dt_bias, a_log, d_skip, ssd_gain,
           w_a, w_b, w_out, post_gain, tabs):
    s, d = x2.shape
    aw = d // 2
    sw = d // 2
    gn = SSD_GROUPS * SSD_STATE
    ssd_heads = sw // SSD_HEAD_DIM
    qi_w = IDX_HEADS * IDX_HEAD_DIM
    sizes = (aw, aw, aw, aw, qi_w, IDX_HEAD_DIM, IDX_HEADS, sw, sw, gn, gn, ssd_heads, 2 * d)
    off = np.concatenate([[0], np.cumsum(sizes)])
    w16 = _cast_bf16(w_in, LANES)
    cols = lambda a, b: w16[:, int(off[a]):int(off[b])]
    assert IDX_HEAD_DIM + IDX_HEADS + ssd_heads <= LANES
    w_ssd = cols(7, 11)
    w_gate = cols(12, 13)
    w_tail = jnp.concatenate(
        [cols(5, 7), cols(11, 12), jnp.zeros((d, 32 - ssd_heads), _BF16)], axis=1)

    h = _rmsnorm(x2, pre_gain, tm=256)
    proj_a = _matmul(h, w16, 1024, 1024, _F32, "in_proj_attn", n=int(off[5]))
    proj_s = _matmul(h, w_ssd, 1024, 1024, _F32, "in_proj_ssd")
    proj_g = _matmul(h, w_gate, 1024, 1024, _F32, "in_proj_gate")
    tail = _matmul(h, w_tail, 1024, LANES, _F32, "in_proj_tail")

    tq_idx = 128
    q, k, v, qi_hm, ki_rot, w_idx = _prep(proj_a, tail, tabs, aw, tq_idx)
    topk = min(TOPK_MAX, s // 4)
    kb = min(512, s)
    bias = _index_bias(qi_hm, ki_rot, w_idx, tq_idx, kb, topk)
    y_a = _attention(q, k, v, bias, proj_a, aw, min(512, s), kb, tq_idx)

    y_b = _ssd(proj_s, tail, conv_w, conv_b, dt_bias, a_log, d_skip, ssd_gain, sw, 0, LANES)

    merged = _merge(y_a, y_b, w_a.astype(_BF16), w_b.astype(_BF16), proj_g, 0, 512, 1024)
    out = _matmul(merged, w_out.astype(_BF16), 1024, 1024, _F32, "out_proj")
    return _postnorm(x2, out, post_gain, tm=256)


def kernel(x, pre_norm_gain, w_in, conv_w, conv_b, dt_bias, a_log, d_skip, ssd_norm_gain,
           w_branch_attn, w_branch_ssd, w_out, post_norm_gain):
    b, s, d = x.shape
    tabs = _rope_tables(s)
    outs = []
    for bi in range(b):
        xb = x[bi]
        for layer in range(pre_norm_gain.shape[0]):
            xb = _layer(xb, pre_norm_gain[layer], w_in[layer], conv_w[layer], conv_b[layer],
                        dt_bias[layer], a_log[layer], d_skip[layer], ssd_norm_gain[layer],
                        w_branch_attn[layer], w_branch_ssd[layer], w_out[layer],
                        post_norm_gain[layer], tabs)
        outs.append(xb)
    return jnp.stack(outs, axis=0)
```

```python
import functools

import numpy as np
import jax
import jax.numpy as jnp
from jax import lax
from jax.experimental import pallas as pl
from jax.experimental.pallas import tpu as pltpu

CHUNK = 64
ROPE_THETA = 10000.0
NORM_EPS = 1e-6
ATTN_HEAD_DIM = 128
IDX_HEADS = 32
IDX_HEAD_DIM = 64
TOPK_MAX = 256
SSD_HEAD_DIM = 64
SSD_GROUPS = 8
SSD_STATE = 128
CONV_WIDTH = 4

LANES = 128
VMEM_LIMIT_BYTES = 56 * 1024 * 1024

NEG_BIAS = -1e30

_F32 = jnp.float32
_BF16 = jnp.bfloat16


def _params(*sem):
    return pltpu.CompilerParams(dimension_semantics=sem, vmem_limit_bytes=VMEM_LIMIT_BYTES)


def _rmsnorm_kernel(x_ref, g_ref, o_ref):
    x = x_ref[...]
    ms = jnp.mean(x * x, axis=-1, keepdims=True)
    o_ref[...] = (x * lax.rsqrt(ms + NORM_EPS) * g_ref[...]).astype(o_ref.dtype)


def _rmsnorm(x, gain, tm):
    s, d = x.shape
    return pl.pallas_call(
        _rmsnorm_kernel,
        grid=(s // tm,),
        in_specs=[pl.BlockSpec((tm, d), lambda i: (i, 0)),
                  pl.BlockSpec((1, d), lambda i: (0, 0))],
        out_specs=pl.BlockSpec((tm, d), lambda i: (i, 0)),
        out_shape=jax.ShapeDtypeStruct((s, d), _BF16),
        compiler_params=_params("parallel"),
        name="pre_rmsnorm",
    )(x, gain.reshape(1, d))


def _matmul_kernel(a_ref, b_ref, o_ref):
    o_ref[...] = jnp.dot(a_ref[...], b_ref[...],
                         preferred_element_type=_F32).astype(o_ref.dtype)


def _cast_kernel(w_ref, o_ref):
    o_ref[...] = w_ref[...].astype(o_ref.dtype)


def _cast_bf16(w, tr):
    rows, n = w.shape
    blk = pl.BlockSpec((tr, n), lambda i: (i, 0))
    return pl.pallas_call(
        _cast_kernel,
        grid=(rows // tr,),
        in_specs=[blk],
        out_specs=blk,
        out_shape=jax.ShapeDtypeStruct((rows, n), _BF16),
        compiler_params=_params("parallel"),
        name="weight_cast",
    )(w)


def _matmul(a, b, tm, tn, out_dtype, name, n=None):
    m, k = a.shape
    n = b.shape[1] if n is None else n
    tm = min(tm, m)
    tn = min(tn, n)
    assert m % tm == 0 and n % tn == 0
    return pl.pallas_call(
        _matmul_kernel,
        grid=(m // tm, n // tn),
        in_specs=[pl.BlockSpec((tm, k), lambda i, j: (i, 0)),
                  pl.BlockSpec((k, tn), lambda i, j: (0, j))],
        out_specs=pl.BlockSpec((tm, tn), lambda i, j: (i, j)),
        out_shape=jax.ShapeDtypeStruct((m, n), out_dtype),
        compiler_params=_params("parallel", "arbitrary"),
        name=name,
    )(a, b)


def _rope_pairs64(x, c4, s4):
    lane = lax.broadcasted_iota(jnp.int32, x.shape, 1)
    first_half = (lane % IDX_HEAD_DIM) < (IDX_HEAD_DIM // 2)
    partner = jnp.where(first_half,
                        pltpu.roll(x, LANES - IDX_HEAD_DIM // 2, 1),
                        pltpu.roll(x, IDX_HEAD_DIM // 2, 1))
    return x * c4 + partner * s4


def _prep_kernel(q_ref, k_ref, v_ref, qi_ref, tail_ref, ca_ref, sa_ref, ci_ref, si_ref,
                 qo_ref, ko_ref, vo_ref, qio_ref, kio_ref, wo_ref, *, heads, w_scale):
    ca = ca_ref[...]
    sa = sa_ref[...]
    for h in range(heads):
        sl = slice(h * LANES, (h + 1) * LANES)
        xq = q_ref[:, sl]
        qo_ref[:, sl] = (xq * ca + pltpu.roll(xq, LANES // 2, 1) * sa).astype(qo_ref.dtype)
        xk = k_ref[:, sl]
        ko_ref[:, sl] = (xk * ca + pltpu.roll(xk, LANES // 2, 1) * sa).astype(ko_ref.dtype)
    ones = jnp.ones((v_ref.shape[0], LANES), vo_ref.dtype)
    for h in range(heads):
        vo_ref[:, 2 * h * LANES:(2 * h + 1) * LANES] = v_ref[:, h * LANES:(h + 1) * LANES].astype(
            vo_ref.dtype)
        vo_ref[:, (2 * h + 1) * LANES:(2 * h + 2) * LANES] = ones
    ci = ci_ref[...]
    si = si_ref[...]
    for g in range(IDX_HEADS // 2):
        r = _rope_pairs64(qi_ref[:, g * LANES:(g + 1) * LANES], ci, si).astype(qio_ref.dtype)
        qio_ref[0, 2 * g] = r[:, :IDX_HEAD_DIM]
        qio_ref[0, 2 * g + 1] = r[:, IDX_HEAD_DIM:]
    t = tail_ref[...]
    kio_ref[...] = _rope_pairs64(t, ci, si)[:, :IDX_HEAD_DIM].astype(kio_ref.dtype)
    wo_ref[...] = t[:, IDX_HEAD_DIM:IDX_HEAD_DIM + IDX_HEADS] * w_scale


def _prep(proj, tail, tabs, aw, tq):
    s = proj.shape[0]
    heads = aw // ATTN_HEAD_DIM
    qi_w = IDX_HEADS * IDX_HEAD_DIM
    qi_blk = (4 * aw) // qi_w
    assert (4 * aw) % qi_w == 0
    nb = s // tq
    row = lambda i: (i, 0)
    kern = functools.partial(_prep_kernel, heads=heads,
                             w_scale=float(IDX_HEADS ** -0.5 * IDX_HEAD_DIM ** -0.5))
    return pl.pallas_call(
        kern,
        grid=(nb,),
        in_specs=[pl.BlockSpec((tq, aw), lambda i: (i, 0)),
                  pl.BlockSpec((tq, aw), lambda i: (i, 1)),
                  pl.BlockSpec((tq, aw), lambda i: (i, 2)),
                  pl.BlockSpec((tq, qi_w), lambda i: (i, qi_blk)),
                  pl.BlockSpec((tq, LANES), row),
                  pl.BlockSpec((tq, LANES), row),
                  pl.BlockSpec((tq, LANES), row),
                  pl.BlockSpec((tq, LANES), row),
                  pl.BlockSpec((tq, LANES), row)],
        out_specs=[pl.BlockSpec((tq, aw), row),
                   pl.BlockSpec((tq, aw), row),
                   pl.BlockSpec((tq, 2 * aw), row),
                   pl.BlockSpec((1, IDX_HEADS, tq, IDX_HEAD_DIM), lambda i: (i, 0, 0, 0)),
                   pl.BlockSpec((tq, IDX_HEAD_DIM), row),
                   pl.BlockSpec((tq, IDX_HEADS), row)],
        out_shape=[jax.ShapeDtypeStruct((s, aw), _BF16),
                   jax.ShapeDtypeStruct((s, aw), _BF16),
                   jax.ShapeDtypeStruct((s, 2 * aw), _BF16),
                   jax.ShapeDtypeStruct((nb, IDX_HEADS, tq, IDX_HEAD_DIM), _BF16),
                   jax.ShapeDtypeStruct((s, IDX_HEAD_DIM), _BF16),
                   jax.ShapeDtypeStruct((s, IDX_HEADS), _F32)],
        compiler_params=_params("parallel"),
        name="rope_prep",
    )(proj, proj, proj, proj, tail, *tabs)


def _index_kernel(qi_ref, ki_ref, w_ref, bias_ref, wb_ref, score_ref, *, tq, kb, nkb_total, topk,
                  head_group):
    i = pl.program_id(0)
    nkb = ((i + 1) * tq + kb - 1) // kb
    ncol = kb // LANES

    w = w_ref[...]
    for h in range(IDX_HEADS):
        wb_ref[h * tq:(h + 1) * tq, :] = jnp.broadcast_to(w[:, h:h + 1], (tq, LANES))

    row = lax.broadcasted_iota(jnp.int32, (tq, LANES), 0)
    lane = lax.broadcasted_iota(jnp.int32, (tq, LANES), 1)
    vis_end = i * tq + (row // CHUNK + 1) * CHUNK

    def score_body(j, carry):
        rmax, rmin = carry
        kj = ki_ref[j]
        acc = [jnp.zeros((tq, LANES), _F32) for _ in range(ncol)]
        for hg in range(IDX_HEADS // head_group):
            q = qi_ref[0, hg * head_group:(hg + 1) * head_group].reshape(
                head_group * tq, IDX_HEAD_DIM)
            logit = lax.dot_general(q, kj, (((1,), (1,)), ((), ())),
                                    preferred_element_type=_F32)
            wb = wb_ref[hg * head_group * tq:(hg + 1) * head_group * tq, :]
            for c in range(ncol):
                r = jnp.maximum(logit[:, c * LANES:(c + 1) * LANES], 0.0) * wb
                acc[c] = acc[c] + jnp.sum(r.reshape(head_group, tq, LANES), axis=0)
        for c in range(ncol):
            admissible = (j * kb + c * LANES + lane) < vis_end
            sc = jnp.where(admissible, acc[c], -jnp.inf)
            score_ref[j, :, c * LANES:(c + 1) * LANES] = sc
            rmax = jnp.maximum(rmax, sc)
            rmin = jnp.minimum(rmin, jnp.where(admissible, acc[c], jnp.inf))
        return rmax, rmin

    rmax, rmin = lax.fori_loop(
        0, nkb, score_body,
        (jnp.full((tq, LANES), -jnp.inf, _F32), jnp.full((tq, LANES), jnp.inf, _F32)))

    kf = float(topk)
    n_adm = vis_end[:, :1].astype(_F32)
    lo0 = jnp.min(rmin, axis=1, keepdims=True)
    hi0 = jnp.max(rmax, axis=1, keepdims=True)
    steps_per_check = 2
    max_checks = 160

    def bisect_step(lo, hi, clo, settled):
        mid = 0.5 * lo + 0.5 * hi
        mid_b = jnp.broadcast_to(mid, (tq, LANES))

        def cnt_body(j, cnt):
            for c in range(ncol):
                sc = score_ref[j, :, c * LANES:(c + 1) * LANES]
                cnt = cnt + jnp.where(sc >= mid_b, 1.0, 0.0)
            return cnt

        cnt = lax.fori_loop(0, nkb, cnt_body, jnp.zeros((tq, LANES), _F32))
        tot = jnp.sum(cnt, axis=1, keepdims=True)
        accept = tot >= kf
        stalled = jnp.logical_or(mid <= lo, mid >= hi)
        lo = jnp.where(accept, mid, lo)
        clo = jnp.where(accept, tot, clo)
        hi = jnp.where(accept, hi, mid)
        settled = jnp.maximum(settled, jnp.where(jnp.logical_or(clo == kf, stalled), 1.0, 0.0))
        return lo, hi, clo, settled

    def check_cond(state):
        n, _, _, _, _, pending = state
        return jnp.logical_and(n < max_checks, pending > 0.0)

    def check_body(state):
        n, lo, hi, clo, settled, _ = state
        for _ in range(steps_per_check):
            lo, hi, clo, settled = bisect_step(lo, hi, clo, settled)
        return n + 1, lo, hi, clo, settled, jnp.max(1.0 - settled)

    settled0 = jnp.where(n_adm <= kf, 1.0, 0.0)
    _, lo, _, _, _, _ = lax.while_loop(
        check_cond, check_body,
        (jnp.int32(0), lo0, hi0, n_adm, settled0, jnp.max(1.0 - settled0)))
    thr = jnp.broadcast_to(jnp.where(n_adm <= kf, jnp.finfo(_F32).min, lo), (tq, LANES))

    def out_body(j, carry):
        for c in range(ncol):
            sc = score_ref[j, :, c * LANES:(c + 1) * LANES]
            bias_ref[0, j, :, c * LANES:(c + 1) * LANES] = jnp.where(sc >= thr, 0.0, NEG_BIAS)
        return carry

    lax.fori_loop(0, nkb, out_body, 0)

    def fill_body(j, carry):
        bias_ref[0, j] = jnp.full((tq, kb), NEG_BIAS, _F32)
        return carry

    lax.fori_loop(nkb, nkb_total, fill_body, 0)


def _index_bias(qi_hm, ki_rot, w, tq, kb, topk):
    nb = qi_hm.shape[0]
    s = ki_rot.shape[0]
    nkb_total = s // kb
    ki3 = ki_rot.reshape(nkb_total, kb, IDX_HEAD_DIM)
    kern = functools.partial(_index_kernel, tq=tq, kb=kb, nkb_total=nkb_total, topk=topk,
                             head_group=8)
    return pl.pallas_call(
        kern,
        grid=(nb,),
        in_specs=[pl.BlockSpec((1, IDX_HEADS, tq, IDX_HEAD_DIM), lambda i: (i, 0, 0, 0)),
                  pl.BlockSpec((nkb_total, kb, IDX_HEAD_DIM), lambda i: (0, 0, 0)),
                  pl.BlockSpec((tq, IDX_HEADS), lambda i: (i, 0))],
        out_specs=pl.BlockSpec((1, nkb_total, tq, kb), lambda i: (i, 0, 0, 0)),
        out_shape=jax.ShapeDtypeStruct((nb, nkb_total, tq, kb), _F32),
        scratch_shapes=[pltpu.VMEM((IDX_HEADS * tq, LANES), _F32),
                        pltpu.VMEM((nkb_total, tq, kb), _F32)],
        compiler_params=_params("parallel"),
        name="index_topk_bias",
    )(qi_hm, ki3, w)


def _attn_kernel(q_ref, k_ref, v_ref, b_ref, g_ref, o_ref, m_ref, l_ref, acc_ref, *,
                 heads, tq, kb, scale):
    i = pl.program_id(0)
    j = pl.program_id(1)
    last = ((i + 1) * tq - 1) // kb
    nrep = kb // LANES

    @pl.when(j == 0)
    def _():
        m_ref[...] = jnp.full(m_ref.shape, NEG_BIAS, _F32)
        l_ref[...] = jnp.zeros(l_ref.shape, _F32)
        acc_ref[...] = jnp.zeros(acc_ref.shape, _F32)

    c = scale * float(np.log2(np.e))

    @pl.when(j <= last)
    def _():
        bias = b_ref[...].reshape(tq, kb)
        for h in range(heads):
            sl = slice(h * ATTN_HEAD_DIM, (h + 1) * ATTN_HEAD_DIM)
            s = lax.dot_general(q_ref[:, sl], k_ref[:, sl], (((1,), (1,)), ((), ())),
                                preferred_element_type=_F32) + bias
            m_prev = m_ref[h]
            m_next = jnp.maximum(m_prev, jnp.max(s, axis=1, keepdims=True))
            alpha = jnp.exp2((m_prev - m_next) * c)
            p = jnp.exp2((s - jnp.concatenate([m_next] * nrep, axis=1)) * c)
            m_ref[h] = m_next
            pv = jnp.dot(p.astype(_BF16), v_ref[:, 2 * h * LANES:(2 * h + 2) * LANES],
                         preferred_element_type=_F32)
            acc_ref[:, sl] = alpha * acc_ref[:, sl] + pv[:, :LANES]
            l_ref[h] = alpha * l_ref[h] + pv[:, LANES:]

    @pl.when(j == last)
    def _():
        for h in range(heads):
            sl = slice(h * ATTN_HEAD_DIM, (h + 1) * ATTN_HEAD_DIM)
            g = g_ref[:, sl]
            o_ref[:, sl] = (acc_ref[:, sl] / l_ref[h] * (g * jax.nn.sigmoid(g))).astype(o_ref.dtype)


def _attention(q, k, v, bias, proj, aw, tq, kb, tq_idx):
    s = q.shape[0]
    heads = aw // ATTN_HEAD_DIM
    nq = s // tq
    nkb = s // kb
    rq = tq // tq_idx
    kmap = lambda i, j: (jnp.minimum(j, ((i + 1) * tq - 1) // kb), 0)
    kern = functools.partial(_attn_kernel, heads=heads, tq=tq, kb=kb,
                             scale=float(ATTN_HEAD_DIM ** -0.5))
    return pl.pallas_call(
        kern,
        grid=(nq, nkb),
        in_specs=[pl.BlockSpec((tq, aw), lambda i, j: (i, 0)),
                  pl.BlockSpec((kb, aw), kmap),
                  pl.BlockSpec((kb, 2 * aw), kmap),
                  pl.BlockSpec((rq, 1, tq_idx, kb),
                               lambda i, j: (i, jnp.minimum(j, ((i + 1) * tq - 1) // kb), 0, 0)),
                  pl.BlockSpec((tq, aw), lambda i, j: (i, 3))],
        out_specs=pl.BlockSpec((tq, aw), lambda i, j: (i, 0)),
        out_shape=jax.ShapeDtypeStruct((s, aw), _BF16),
        scratch_shapes=[pltpu.VMEM((heads, tq, LANES), _F32),
                        pltpu.VMEM((heads, tq, LANES), _F32),
                        pltpu.VMEM((tq, aw), _F32)],
        compiler_params=_params("parallel", "arbitrary"),
        name="masked_attention",
    )(q, k, v, bias, proj)


def _softplus(x):
    return jnp.maximum(x, 0.0) + jnp.log1p(jnp.exp(-jnp.abs(x)))


def _silu(x):
    return x * jax.nn.sigmoid(x)


def _dot_f32(a, b):
    return jnp.dot(a, b, preferred_element_type=_F32, precision=lax.Precision.HIGHEST)


def _select_columns(x, onehot16):
    x1 = x.astype(_BF16)
    r1 = x - x1.astype(_F32)
    x2 = r1.astype(_BF16)
    x3 = (r1 - x2.astype(_F32)).astype(_BF16)
    dot = lambda a: jnp.dot(a, onehot16, preferred_element_type=_F32)
    return dot(x1) + dot(x2) + dot(x3)


def _ssd_kernel(z_ref, xs_ref, bm_ref, cm_ref, tail_ref, cw_ref, cb_ref, dtb_ref, alog_ref,
                dsk_ref, gain_ref, eh_ref, tri_ref, o_ref,
                xs_ext, bm_ext, cm_ext, st_ref, *, blk, sw, gn, heads_per_group):
    i = pl.program_id(0)
    gw = heads_per_group * SSD_HEAD_DIM
    halo = 8

    @pl.when(i == 0)
    def _():
        xs_ext[0:halo, :] = jnp.zeros((halo, sw), _F32)
        bm_ext[0:halo, :] = jnp.zeros((halo, gn), _F32)
        cm_ext[0:halo, :] = jnp.zeros((halo, gn), _F32)
        st_ref[...] = jnp.zeros(st_ref.shape, _F32)

    def conv(ext, src_ref, off, width):
        ext[halo:halo + blk, :] = src_ref[...]
        acc = cb_ref[:, off:off + width]
        for t in range(CONV_WIDTH):
            start = halo - (CONV_WIDTH - 1) + t
            acc = acc + ext[start:start + blk, :] * cw_ref[t:t + 1, off:off + width]
        ext[0:halo, :] = ext[blk:blk + halo, :]
        return _silu(acc)

    xs_c = conv(xs_ext, xs_ref, 0, sw)
    bm_c = conv(bm_ext, bm_ref, sw, gn)
    cm_c = conv(cm_ext, cm_ref, sw + gn, gn)

    dt = _softplus(tail_ref[...] + dtb_ref[...])
    dta = dt * (-jnp.exp(alog_ref[...]))
    a_cum = _dot_f32(tri_ref[...], dta)
    a_cum_t = a_cum.T
    dt_e = _select_columns(dt, eh_ref[...])
    acum_e = _select_columns(a_cum, eh_ref[...])
    a_last_e = acum_e[blk - 1:blk, :]

    xdt = xs_c * dt_e
    xw = (xdt * jnp.exp(a_last_e - acum_e)).astype(_BF16)
    xdt16 = xdt.astype(_BF16)
    exp_acum = jnp.exp(acum_e)
    bm16 = bm_c.astype(_BF16)
    cm16 = cm_c.astype(_BF16)

    rowi = lax.broadcasted_iota(jnp.int32, (blk, blk), 0)
    coli = lax.broadcasted_iota(jnp.int32, (blk, blk), 1)
    causal = rowi >= coli
    lane_g = lax.broadcasted_iota(jnp.int32, (blk, gw), 1)

    for g in range(SSD_GROUPS):
        bg = bm16[:, g * SSD_STATE:(g + 1) * SSD_STATE]
        cg = cm16[:, g * SSD_STATE:(g + 1) * SSD_STATE]
        cbm = lax.dot_general(cg, bg, (((1,), (1,)), ((), ())), preferred_element_type=_F32)
        x_g = xdt16[:, g * gw:(g + 1) * gw]
        lhs = []
        rhs = []
        for r in range(heads_per_group):
            h = g * heads_per_group + r
            col = LANES - 32 + h
            seg = (jnp.broadcast_to(a_cum[:, col:col + 1], (blk, blk))
                   - a_cum_t[col:col + 1, :])
            decay = jnp.where(causal, jnp.exp(seg), 0.0)
            lhs.append((cbm * decay).astype(_BF16))
            in_head = (lane_g >= r * SSD_HEAD_DIM) & (lane_g < (r + 1) * SSD_HEAD_DIM)
            rhs.append(jnp.where(in_head, x_g, jnp.zeros_like(x_g)))
        y_diag = jnp.dot(jnp.concatenate(lhs, axis=1), jnp.concatenate(rhs, axis=0),
                         preferred_element_type=_F32)
        st = st_ref[g]
        y_off = jnp.dot(cg, st.astype(_BF16), preferred_element_type=_F32)
        sl = slice(g * gw, (g + 1) * gw)
        y = y_diag + y_off * exp_acum[:, sl] + xs_c[:, sl] * dsk_ref[:, sl]
        yg = y * _silu(z_ref[:, sl])
        ms = jnp.mean(yg * yg, axis=-1, keepdims=True)
        o_ref[:, sl] = (yg * lax.rsqrt(ms + NORM_EPS) * gain_ref[:, sl]).astype(o_ref.dtype)
        bg_t = bm_c[:, g * SSD_STATE:(g + 1) * SSD_STATE].T.astype(_BF16)
        upd = jnp.dot(bg_t, xw[:, sl], preferred_element_type=_F32)
        st_ref[g] = st * jnp.exp(a_last_e[:, sl]) + upd


def _ssd(proj, tail, conv_w, conv_b, dt_bias, a_log, d_skip, norm_gain, sw, col_z, blk):
    s = proj.shape[0]
    heads = sw // SSD_HEAD_DIM
    hpg = heads // SSD_GROUPS
    gn = SSD_GROUPS * SSD_STATE
    gw = hpg * SSD_HEAD_DIM
    assert blk == LANES and heads <= 32 and sw // SSD_GROUPS == gw
    conv_ch = sw + 2 * gn
    pad = LANES - 32
    place = lambda v: jnp.zeros((1, LANES), _F32).at[0, pad:pad + heads].set(v)
    dtb = place(dt_bias)
    alog = place(a_log)
    dsk = jnp.repeat(d_skip, SSD_HEAD_DIM).reshape(1, sw)
    eh = np.zeros((LANES, sw), np.float32)
    for h in range(heads):
        eh[pad + h, h * SSD_HEAD_DIM:(h + 1) * SSD_HEAD_DIM] = 1.0
    tri = np.tril(np.ones((blk, blk), np.float32))
    assert col_z % sw == 0 and (col_z + 2 * sw) % gn == 0
    z_blk = col_z // sw
    b_blk = (col_z + 2 * sw) // gn
    full = lambda shape: pl.BlockSpec(shape, lambda i: (0, 0))
    kern = functools.partial(_ssd_kernel, blk=blk, sw=sw, gn=gn, heads_per_group=hpg)
    return pl.pallas_call(
        kern,
        grid=(s // blk,),
        in_specs=[pl.BlockSpec((blk, sw), lambda i: (i, z_blk)),
                  pl.BlockSpec((blk, sw), lambda i: (i, z_blk + 1)),
                  pl.BlockSpec((blk, gn), lambda i: (i, b_blk)),
                  pl.BlockSpec((blk, gn), lambda i: (i, b_blk + 1)),
                  pl.BlockSpec((blk, LANES), lambda i: (i, 0)),
                  full((CONV_WIDTH, conv_ch)), full((1, conv_ch)),
                  full((1, LANES)), full((1, LANES)), full((1, sw)), full((1, sw)),
                  full((LANES, sw)), full((blk, blk))],
        out_specs=pl.BlockSpec((blk, sw), lambda i: (i, 0)),
        out_shape=jax.ShapeDtypeStruct((s, sw), _BF16),
        scratch_shapes=[pltpu.VMEM((blk + 8, sw), _F32),
                        pltpu.VMEM((blk + 8, gn), _F32),
                        pltpu.VMEM((blk + 8, gn), _F32),
                        pltpu.VMEM((SSD_GROUPS, SSD_STATE, gw), _F32)],
        compiler_params=_params("arbitrary"),
        name="ssd_mixer",
    )(proj, proj, proj, proj, tail, conv_w, conv_b.reshape(1, conv_ch), dtb, alog, dsk,
      norm_gain.reshape(1, sw), jnp.asarray(eh, _BF16), jnp.asarray(tri))


def _merge_kernel(ya_ref, yb_ref, wa_ref, wb_ref, la_ref, lb_ref, o_ref):
    pa = jnp.dot(ya_ref[...], wa_ref[...], preferred_element_type=_F32)
    pb = jnp.dot(yb_ref[...], wb_ref[...], preferred_element_type=_F32)
    o_ref[...] = (jax.nn.sigmoid(la_ref[...]) * pa
                  + jax.nn.sigmoid(lb_ref[...]) * pb).astype(o_ref.dtype)


def _merge(ya, yb, wa, wb, proj, col_merge, tm, tn):
    s, aw = ya.shape
    d = wa.shape[1]
    tm = min(tm, s)
    tn = min(tn, d)
    assert col_merge % tn == 0 and d % tn == 0
    ca = col_merge // tn
    cb = (col_merge + d) // tn
    return pl.pallas_call(
        _merge_kernel,
        grid=(s // tm, d // tn),
        in_specs=[pl.BlockSpec((tm, aw), lambda i, j: (i, 0)),
                  pl.BlockSpec((tm, yb.shape[1]), lambda i, j: (i, 0)),
                  pl.BlockSpec((aw, tn), lambda i, j: (0, j)),
                  pl.BlockSpec((yb.shape[1], tn), lambda i, j: (0, j)),
                  pl.BlockSpec((tm, tn), lambda i, j: (i, ca + j)),
                  pl.BlockSpec((tm, tn), lambda i, j: (i, cb + j))],
        out_specs=pl.BlockSpec((tm, tn), lambda i, j: (i, j)),
        out_shape=jax.ShapeDtypeStruct((s, d), _BF16),
        compiler_params=_params("parallel", "arbitrary"),
        name="gated_merge",
    )(ya, yb, wa, wb, proj, proj)


def _outproj_norm_kernel(a_ref, w_ref, x_ref, g_ref, o_ref, *, tn, nt):
    j = pl.program_id(1)
    col = pl.multiple_of(j * tn, tn)
    o_ref[:, pl.ds(col, tn)] = jnp.dot(a_ref[...], w_ref[...], preferred_element_type=_F32)

    @pl.when(j == nt - 1)
    def _():
        o = o_ref[...]
        ms = jnp.mean(o * o, axis=-1, keepdims=True)
        o_ref[...] = x_ref[...] + o * lax.rsqrt(ms + NORM_EPS) * g_ref[...]


def _outproj_norm(a, w, x, gain, tm, tn):
    s, d = x.shape
    k = a.shape[1]
    tm = min(tm, s)
    tn = min(tn, d)
    nt = d // tn
    assert s % tm == 0 and d % tn == 0
    row = pl.BlockSpec((tm, d), lambda i, j: (i, 0))
    kern = functools.partial(_outproj_norm_kernel, tn=tn, nt=nt)
    return pl.pallas_call(
        kern,
        grid=(s // tm, nt),
        in_specs=[pl.BlockSpec((tm, k), lambda i, j: (i, 0)),
                  pl.BlockSpec((k, tn), lambda i, j: (0, j)),
                  row,
                  pl.BlockSpec((1, d), lambda i, j: (0, 0))],
        out_specs=row,
        out_shape=jax.ShapeDtypeStruct((s, d), _F32),
        compiler_params=_params("parallel", "arbitrary"),
        name="out_proj_postnorm",
    )(a, w, x, gain.reshape(1, d))


def _rope_tables(seq_len):
    def tab(dim):
        inv_freq = 1.0 / (ROPE_THETA ** (jnp.arange(0, dim, 2, dtype=_F32) / dim))
        ang = jnp.arange(seq_len, dtype=_F32)[:, None] * inv_freq[None, :]
        return jnp.cos(ang), jnp.sin(ang)
    ca, sa = tab(ATTN_HEAD_DIM)
    ci, si = tab(IDX_HEAD_DIM)
    return (jnp.concatenate([ca, ca], axis=1), jnp.concatenate([-sa, sa], axis=1),
            jnp.concatenate([ci, ci, ci, ci], axis=1), jnp.concatenate([-si, si, -si, si], axis=1))


def _layer(x2, pre_gain, w_in, conv_w, conv_b, dt_bias, a_log, d_skip, ssd_gain,
           w_a, w_b, w_out, post_gain, tabs):
    s, d = x2.shape
    aw = d // 2
    sw = d // 2
    gn = SSD_GROUPS * SSD_STATE
    ssd_heads = sw // SSD_HEAD_DIM
    qi_w = IDX_HEADS * IDX_HEAD_DIM
    sizes = (aw, aw, aw, aw, qi_w, IDX_HEAD_DIM, IDX_HEADS, sw, sw, gn, gn, ssd_heads, 2 * d)
    off = np.concatenate([[0], np.cumsum(sizes)])
    w16 = _cast_bf16(w_in, LANES)
    cols = lambda a, b: w16[:, int(off[a]):int(off[b])]
    assert IDX_HEAD_DIM + IDX_HEADS + ssd_heads <= LANES
    w_ssd = cols(7, 11)
    w_gate = cols(12, 13)
    w_tail = jnp.concatenate(
        [cols(5, 7), cols(11, 12), jnp.zeros((d, 32 - ssd_heads), _BF16)], axis=1)

    h = _rmsnorm(x2, pre_gain, tm=256)
    proj_a = _matmul(h, w16, 1024, 1024, _F32, "in_proj_attn", n=int(off[5]))
    proj_s = _matmul(h, w_ssd, 1024, 1024, _F32, "in_proj_ssd")
    proj_g = _matmul(h, w_gate, 1024, 1024, _F32, "in_proj_gate")
    tail = _matmul(h, w_tail, 1024, LANES, _F32, "in_proj_tail")

    tq_idx = 128
    q, k, v, qi_hm, ki_rot, w_idx = _prep(proj_a, tail, tabs, aw, tq_idx)
    topk = min(TOPK_MAX, s // 4)
    kb = min(512, s)
    bias = _index_bias(qi_hm, ki_rot, w_idx, tq_idx, kb, topk)
    y_a = _attention(q, k, v, bias, proj_a, aw, min(512, s), kb, tq_idx)

    y_b = _ssd(proj_s, tail, conv_w, conv_b, dt_bias, a_log, d_skip, ssd_gain, sw, 0, LANES)

    merged = _merge(y_a, y_b, w_a.astype(_BF16), w_b.astype(_BF16), proj_g, 0, 512, 1024)
    return _outproj_norm(merged, w_out.astype(_BF16), x2, post_gain, 512, 512)


def kernel(x, pre_norm_gain, w_in, conv_w, conv_b, dt_bias, a_log, d_skip, ssd_norm_gain,
           w_branch_attn, w_branch_ssd, w_out, post_norm_gain):
    b, s, d = x.shape
    tabs = _rope_tables(s)
    outs = []
    for bi in range(b):
        xb = x[bi]
        for layer in range(pre_norm_gain.shape[0]):
            xb = _layer(xb, pre_norm_gain[layer], w_in[layer], conv_w[layer], conv_b[layer],
                        dt_bias[layer], a_log[layer], d_skip[layer], ssd_norm_gain[layer],
                        w_branch_attn[layer], w_branch_ssd[layer], w_out[layer],
                        post_norm_gain[layer], tabs)
        outs.append(xb)
    return jnp.stack(outs, axis=0)
```
